```python
import jax, jax.numpy as jnp
from jax import lax
import numpy as np

D_MODEL = 1024
BATCH = 2
SEQ = 16384
DEPTH = 1

HEAD_DIM = 128
N_Q_HEADS = D_MODEL // HEAD_DIM
N_KV_HEADS = 2
D_ATTN = N_Q_HEADS * HEAD_DIM
D_KV = N_KV_HEADS * HEAD_DIM
WINDOW = 128
BLOCK = 128
ROPE_THETA = 10000.0
D_RNN = (5 * D_MODEL) // 4
RNN_BLOCK_W = 128
N_RNN_BLOCKS = D_RNN // RNN_BLOCK_W
CONV_W = 4
CONV_LEFT = 2
LRU_C = 8.0
N_DIRS = 2
ALPHA = (2.0 * DEPTH) ** 0.25
BETA = (8.0 * DEPTH) ** -0.25
LN_EPS = 1e-5

kernel_name = "hybrid_swa_rglru_deepnorm_encoder"


def _rope(t, pos):
    half = HEAD_DIM // 2
    inv = ROPE_THETA ** (-jnp.arange(half, dtype=jnp.float32) * (2.0 / HEAD_DIM))
    ang = pos.astype(jnp.float32)[:, None] * inv[None, :]
    cos = jnp.cos(ang)[None, :, None, :]
    sin = jnp.sin(ang)[None, :, None, :]
    t32 = t.astype(jnp.float32)
    t1, t2 = t32[..., :half], t32[..., half:]
    out = jnp.concatenate([t1 * cos - t2 * sin, t2 * cos + t1 * sin], axis=-1)
    return out.astype(t.dtype)


def _window_attention(q, k, v, sink):
    B, S = q.shape[0], q.shape[1]
    nb = S // BLOCK
    G = N_Q_HEADS // N_KV_HEADS
    qb = q.astype(jnp.float32).reshape(B, nb, BLOCK, N_KV_HEADS, G, HEAD_DIM)

    def band(t):
        tp = jnp.pad(t.astype(jnp.float32).reshape(B, nb, BLOCK, N_KV_HEADS, HEAD_DIM),
                     ((0, 0), (1, 1), (0, 0), (0, 0), (0, 0)))
        return jnp.concatenate([tp[:, :-2], tp[:, 1:-1], tp[:, 2:]], axis=2)

    kb, vb = band(k), band(v)
    s = jnp.einsum('bnqhgd,bnkhd->bnhgqk', qb, kb) * (HEAD_DIM ** -0.5)
    blk = jnp.arange(nb)[:, None]
    qpos = blk * BLOCK + jnp.arange(BLOCK)[None, :]
    kpos = (blk - 1) * BLOCK + jnp.arange(3 * BLOCK)[None, :]
    valid = ((jnp.abs(qpos[:, :, None] - kpos[:, None, :]) <= WINDOW)
             & (kpos[:, None, :] >= 0) & (kpos[:, None, :] < S))
    s = jnp.where(valid[None, :, None, None], s, -jnp.inf)
    sk = sink.astype(jnp.float32).reshape(N_KV_HEADS, G)[None, None, :, :, None, None]
    m = jnp.maximum(jnp.max(s, axis=-1, keepdims=True), sk)
    p = jnp.exp(s - m)
    denom = jnp.sum(p, axis=-1, keepdims=True) + jnp.exp(sk - m)
    o = jnp.einsum('bnhgqk,bnkhd->bnqhgd', p / denom, vb)
    return o.reshape(B, S, N_Q_HEADS * HEAD_DIM).astype(q.dtype)


def _centred_depthwise_conv(u, w, b):
    S = u.shape[1]
    up = jnp.pad(u, ((0, 0), (CONV_LEFT, CONV_W - 1 - CONV_LEFT), (0, 0)))
    y = b
    for j in range(CONV_W):
        y = y + w[j] * up[:, j:j + S]
    return y


def _linear_scan(a, u, reverse):
    def step(h, au):
        a_t, u_t = au
        h = a_t * h + u_t
        return h, h
    h0 = jnp.zeros((a.shape[0], a.shape[2]), jnp.float32)
    _, hs = lax.scan(step, h0, (jnp.swapaxes(a, 0, 1), jnp.swapaxes(u, 0, 1)), reverse=reverse)
    return jnp.swapaxes(hs, 0, 1)


def _bidirectional_rglru(xc, wa, ba, wx, bx, lam):
    B, S = xc.shape[0], xc.shape[1]
    x32 = xc.astype(jnp.float32)
    xb = x32.reshape(B, S, N_RNN_BLOCKS, RNN_BLOCK_W)
    r = jax.nn.sigmoid(jnp.einsum('bsni,dnij->bsdnj', xb, wa.astype(jnp.float32))
                       .reshape(B, S, N_DIRS, D_RNN) + ba.astype(jnp.float32))
    i = jax.nn.sigmoid(jnp.einsum('bsni,dnij->bsdnj', xb, wx.astype(jnp.float32))
                       .reshape(B, S, N_DIRS, D_RNN) + bx.astype(jnp.float32))
    log_a = -LRU_C * r * jax.nn.softplus(-lam.astype(jnp.float32))
    a = jnp.exp(log_a)
    u = jnp.sqrt(-jnp.expm1(2.0 * log_a)) * (i * x32[:, :, None, :])
    h_fwd = _linear_scan(a[:, :, 0], u[:, :, 0], reverse=False)
    h_bwd = _linear_scan(a[:, :, 1], u[:, :, 1], reverse=True)
    return (h_fwd + h_bwd).astype(xc.dtype)


def _layernorm(x, g, b):
    x32 = x.astype(jnp.float32)
    mu = jnp.mean(x32, axis=-1, keepdims=True)
    var = jnp.mean(jnp.square(x32 - mu), axis=-1, keepdims=True)
    y = (x32 - mu) * lax.rsqrt(var + LN_EPS) * g.astype(jnp.float32) + b.astype(jnp.float32)
    return y.astype(x.dtype)


def setup_inputs(seed: int = 0) -> dict:
    key = jax.random.key(seed)
    ks = jax.random.split(key, 20)
    total = 2 * D_ATTN + 2 * D_KV + 2 * D_RNN + 2 * D_MODEL
    f32 = jnp.float32
    x = jax.random.normal(ks[0], (BATCH, SEQ, D_MODEL), f32)
    col_scale = jnp.concatenate([
        jnp.ones((D_ATTN + D_KV,), f32),
        jnp.full((D_KV,), BETA, f32),
        jnp.ones((D_ATTN + 2 * D_RNN + 2 * D_MODEL,), f32)])
    w_in = jax.random.normal(ks[1], (DEPTH, D_MODEL, total), f32) * (D_MODEL ** -0.5) * col_scale
    b_in = 0.01 * jax.random.normal(ks[2], (DEPTH, total), f32)
    attn_sink = 0.5 * jax.random.normal(ks[3], (DEPTH, N_Q_HEADS), f32)
    conv_w = 0.5 * jax.random.normal(ks[4], (DEPTH, CONV_W, D_RNN), f32)
    conv_b = 0.01 * jax.random.normal(ks[5], (DEPTH, D_RNN), f32)
    gshape = (DEPTH, N_DIRS, N_RNN_BLOCKS, RNN_BLOCK_W, RNN_BLOCK_W)
    lru_wa = jax.random.normal(ks[6], gshape, f32) * (RNN_BLOCK_W ** -0.5)
    lru_ba = 0.01 * jax.random.normal(ks[7], (DEPTH, N_DIRS, D_RNN), f32)
    lru_wx = jax.random.normal(ks[8], gshape, f32) * (RNN_BLOCK_W ** -0.5)
    lru_bx = 0.01 * jax.random.normal(ks[9], (DEPTH, N_DIRS, D_RNN), f32)
    a_c = jax.random.uniform(ks[10], (DEPTH, N_DIRS, D_RNN), f32, 0.9, 0.999)
    a0 = a_c ** (1.0 / LRU_C)
    lru_lambda = jnp.log(a0) - jnp.log1p(-a0)
    w_branch_attn = jax.random.normal(ks[11], (DEPTH, D_ATTN, D_MODEL), f32) * (D_ATTN ** -0.5) * BETA
    w_branch_rnn = jax.random.normal(ks[12], (DEPTH, D_RNN, D_MODEL), f32) * (D_RNN ** -0.5) * BETA
    w_out = jax.random.normal(ks[13], (DEPTH, D_MODEL, D_MODEL), f32) * (D_MODEL ** -0.5) * BETA
    b_out = 0.01 * jax.random.normal(ks[14], (DEPTH, D_MODEL), f32)
    ln_gain = 1.0 + 0.02 * jax.random.normal(ks[15], (DEPTH, D_MODEL), f32)
    ln_bias = 0.02 * jax.random.normal(ks[16], (DEPTH, D_MODEL), f32)
    return {"x": x, "w_in": w_in, "b_in": b_in, "attn_sink": attn_sink,
            "conv_w": conv_w, "conv_b": conv_b, "lru_wa": lru_wa, "lru_ba": lru_ba,
            "lru_wx": lru_wx, "lru_bx": lru_bx, "lru_lambda": lru_lambda,
            "w_branch_attn": w_branch_attn, "w_branch_rnn": w_branch_rnn,
            "w_out": w_out, "b_out": b_out, "ln_gain": ln_gain, "ln_bias": ln_bias}


def reference(x, w_in, b_in, attn_sink, conv_w, conv_b, lru_wa, lru_ba, lru_wx, lru_bx,
              lru_lambda, w_branch_attn, w_branch_rnn, w_out, b_out, ln_gain, ln_bias):
    B, S = x.shape[0], x.shape[1]
    sizes = (D_ATTN, D_KV, D_KV, D_ATTN, D_RNN, D_RNN, D_MODEL, D_MODEL)
    split_at = [int(c) for c in np.cumsum(sizes)[:-1]]
    pos = jnp.arange(S)
    for l in range(DEPTH):
        proj = jnp.einsum('bsd,de->bse', x, w_in[l]) + b_in[l]
        q, k, v, g_attn, x_rnn, g_rnn, m_attn, m_rnn = jnp.split(proj, split_at, axis=-1)
        q = _rope(q.reshape(B, S, N_Q_HEADS, HEAD_DIM), pos)
        k = _rope(k.reshape(B, S, N_KV_HEADS, HEAD_DIM), pos)
        v = v.reshape(B, S, N_KV_HEADS, HEAD_DIM)
        attn = _window_attention(q, k, v, attn_sink[l]) * jax.nn.silu(g_attn)
        xc = _centred_depthwise_conv(x_rnn, conv_w[l], conv_b[l])
        rnn = _bidirectional_rglru(xc, lru_wa[l], lru_ba[l], lru_wx[l], lru_bx[l],
                                   lru_lambda[l]) * jax.nn.silu(g_rnn)
        merged = (jax.nn.sigmoid(m_attn) * jnp.einsum('bse,ed->bsd', attn, w_branch_attn[l])
                  + jax.nn.sigmoid(m_rnn) * jnp.einsum('bse,ed->bsd', rnn, w_branch_rnn[l]))
        out = jnp.einsum('bsd,de->bse', merged, w_out[l]) + b_out[l]
        x = _layernorm(ALPHA * x + out, ln_gain[l], ln_bias[l])
    return x
```

```python
import functools

import jax
import jax.numpy as jnp
from jax import lax
from jax.experimental import pallas as pl
from jax.experimental.pallas import tpu as pltpu

D_MODEL = 1024
HEAD_DIM = 128
N_Q_HEADS = D_MODEL // HEAD_DIM
N_KV_HEADS = 2
Q_PER_KV = N_Q_HEADS // N_KV_HEADS
D_ATTN = N_Q_HEADS * HEAD_DIM
D_KV = N_KV_HEADS * HEAD_DIM
WINDOW = 128
BLOCK = 128
ROPE_THETA = 10000.0
D_RNN = (5 * D_MODEL) // 4
RNN_BLOCK_W = 128
N_RNN_BLOCKS = D_RNN // RNN_BLOCK_W
CONV_W = 4
CONV_LEFT = 2
LRU_C = 8.0
LN_EPS = 1e-5

LANES = 128
SUBLANES = 8
MXU_WIDTH = 256
VMEM_LIMIT_BYTES = 56 * 1024 * 1024

F32 = jnp.float32
BF16 = jnp.bfloat16

SECTION_SIZES = (D_ATTN, D_KV, D_KV, D_ATTN, D_RNN, D_RNN, D_MODEL, D_MODEL)
SECTION_KINDS = ("rope_q", "rope", "none", "silu", "none", "silu", "sigmoid", "sigmoid")
SECTION_DTYPES = (BF16, BF16, BF16, F32, F32, F32, F32, F32)

TM_INPROJ = 512
TQ_ATTN = 512
TB_RNN = 256
TM_TAIL = 256


_sigmoid = jax.nn.sigmoid


def _inproj_body(x_ref, w_ref, b_ref, cos_ref, sin_ref, *out_refs):
    xb = x_ref[...]
    cos = cos_ref[...]
    sin = sin_ref[...]

    def rope(t):
        return t * cos + pltpu.roll(t, HEAD_DIM // 2, 1) * sin

    start = 0
    for size, kind, out_ref in zip(SECTION_SIZES, SECTION_KINDS, out_refs):
        for c in range(0, size, MXU_WIDTH):
            col = start + c
            t = jnp.dot(xb, w_ref[:, col:col + MXU_WIDTH], preferred_element_type=F32)
            t = t + b_ref[:, col:col + MXU_WIDTH]
            if kind in ("rope", "rope_q"):
                scale = HEAD_DIM ** -0.5 if kind == "rope_q" else 1.0
                for h in range(MXU_WIDTH // HEAD_DIM):
                    th = rope(t[:, h * HEAD_DIM:(h + 1) * HEAD_DIM]) * scale
                    out_ref[:, c + h * HEAD_DIM:c + (h + 1) * HEAD_DIM] = th.astype(out_ref.dtype)
                continue
            if kind == "silu":
                t = t * _sigmoid(t)
            elif kind == "sigmoid":
                t = _sigmoid(t)
            out_ref[:, c:c + MXU_WIDTH] = t.astype(out_ref.dtype)
        start += size


def _inproj(xb, w, b, cos_t, sin_t, seq):
    tokens = xb.shape[0]
    total = w.shape[1]
    tm = TM_INPROJ
    pos_blocks = seq // tm
    out_shape = tuple(jax.ShapeDtypeStruct((tokens, n), dt)
                      for n, dt in zip(SECTION_SIZES, SECTION_DTYPES))
    out_specs = tuple(pl.BlockSpec((tm, n), lambda i: (i, 0)) for n in SECTION_SIZES)
    return pl.pallas_call(
        _inproj_body,
        grid=(tokens // tm,),
        in_specs=[
            pl.BlockSpec((tm, D_MODEL), lambda i: (i, 0)),
            pl.BlockSpec((D_MODEL, total), lambda i: (0, 0), pipeline_mode=pl.Buffered(1)),
            pl.BlockSpec((1, total), lambda i: (0, 0)),
            pl.BlockSpec((tm, HEAD_DIM), lambda i: (i % pos_blocks, 0)),
            pl.BlockSpec((tm, HEAD_DIM), lambda i: (i % pos_blocks, 0)),
        ],
        out_specs=out_specs,
        out_shape=out_shape,
        compiler_params=pltpu.CompilerParams(
            dimension_semantics=("arbitrary",), vmem_limit_bytes=VMEM_LIMIT_BYTES),
        name="inproj",
    )(xb, w, b, cos_t, sin_t)


def _attn_body(sink_ref, q_ref, kp_ref, kc_ref, kn_ref, vp_ref, vc_ref, vn_ref, g_ref,
               o_ref, kband_ref, vband_ref):
    i = pl.program_id(1)
    last = pl.num_programs(1) - 1
    tq = q_ref.shape[0]
    nblk = tq // BLOCK
    band_w = 3 * BLOCK

    kband_ref[0:BLOCK, :] = kp_ref[...]
    kband_ref[BLOCK:BLOCK + tq, :] = kc_ref[...]
    kband_ref[BLOCK + tq:, :] = kn_ref[...]
    vband_ref[0:BLOCK, :] = vp_ref[...]
    vband_ref[BLOCK:BLOCK + tq, :] = vc_ref[...]
    vband_ref[BLOCK + tq:, :] = vn_ref[...]

    row = lax.broadcasted_iota(jnp.int32, (BLOCK, band_w), 0)
    col = lax.broadcasted_iota(jnp.int32, (BLOCK, band_w), 1)
    in_band = (col >= row + (BLOCK - WINDOW)) & (col <= row + (BLOCK + WINDOW))

    for j in range(nblk):
        valid = in_band
        if j == 0:
            valid = valid & ((col >= BLOCK) | (i > 0))
        if j == nblk - 1:
            valid = valid & ((col < 2 * BLOCK) | (i < last))
        rows = slice(j * BLOCK, (j + 1) * BLOCK)
        for h in range(N_KV_HEADS):
            kj = kband_ref[j * BLOCK:j * BLOCK + band_w, h * HEAD_DIM:(h + 1) * HEAD_DIM]
            vj = vband_ref[j * BLOCK:j * BLOCK + band_w, h * HEAD_DIM:(h + 1) * HEAD_DIM]
            heads = [h * Q_PER_KV + g for g in range(Q_PER_KV)]
            q4 = jnp.concatenate(
                [q_ref[rows, hq * HEAD_DIM:(hq + 1) * HEAD_DIM] for hq in heads], axis=0)
            s4 = lax.dot_general(q4, kj, (((1,), (1,)), ((), ())),
                                 preferred_element_type=F32)
            ps, inv = [], []
            for g, hq in enumerate(heads):
                s = jnp.where(valid, s4[g * BLOCK:(g + 1) * BLOCK], -jnp.inf)
                sk = sink_ref[hq]
                m = jnp.maximum(jnp.max(s, axis=-1, keepdims=True), sk)
                p = jnp.exp(s - m)
                denom = jnp.sum(p, axis=-1, keepdims=True) + jnp.exp(sk - m)
                ps.append(p.astype(BF16))
                inv.append(1.0 / denom)
            o4 = jnp.dot(jnp.concatenate(ps, axis=0), vj, preferred_element_type=F32)
            for g, hq in enumerate(heads):
                cols = slice(hq * HEAD_DIM, (hq + 1) * HEAD_DIM)
                o = o4[g * BLOCK:(g + 1) * BLOCK] * inv[g]
                o_ref[rows, cols] = (o * g_ref[rows, cols]).astype(o_ref.dtype)


def _attention(sink, q, k, v, gate):
    batch, seq = q.shape[0], q.shape[1]
    tq = TQ_ATTN
    per = tq // BLOCK
    n_halo = seq // BLOCK

    def cur(b, i):
        return (b, i, 0)

    def prev(b, i):
        return (b, jnp.maximum(i * per - 1, 0), 0)

    def nxt(b, i):
        return (b, jnp.minimum((i + 1) * per, n_halo - 1), 0)

    kv_cur = pl.BlockSpec((None, tq, D_KV), cur)
    kv_prev = pl.BlockSpec((None, BLOCK, D_KV), prev)
    kv_next = pl.BlockSpec((None, BLOCK, D_KV), nxt)
    return pl.pallas_call(
        _attn_body,
        grid=(batch, seq // tq),
        in_specs=[
            pl.BlockSpec(memory_space=pltpu.SMEM),
            pl.BlockSpec((None, tq, D_ATTN), cur),
            kv_prev, kv_cur, kv_next,
            kv_prev, kv_cur, kv_next,
            pl.BlockSpec((None, tq, D_ATTN), cur),
        ],
        out_specs=pl.BlockSpec((None, tq, D_ATTN), cur),
        out_shape=jax.ShapeDtypeStruct((batch, seq, D_ATTN), BF16),
        scratch_shapes=[pltpu.VMEM((tq + 2 * BLOCK, D_KV), BF16),
                        pltpu.VMEM((tq + 2 * BLOCK, D_KV), BF16)],
        compiler_params=pltpu.CompilerParams(
            dimension_semantics=("arbitrary", "arbitrary"), vmem_limit_bytes=VMEM_LIMIT_BYTES),
        name="band_attention",
    )(sink, q, k, k, k, v, v, v, gate)


def _rnn_direction(reverse, first, last, xp_ref, xc_ref, xn_ref, cw_ref, cb_ref, w_ref,
                   ba_ref, bx_ref, lam_ref, h_ref, ext_ref, a_ref, u_ref, carry_ref):
    tb = xc_ref.shape[0]
    halo = SUBLANES
    ext_ref[0:halo, :] = jnp.where(first, 0.0, xp_ref[...])
    ext_ref[halo:halo + tb, :] = xc_ref[...]
    ext_ref[halo + tb:, :] = jnp.where(last, 0.0, xn_ref[...])

    xc = cb_ref[...]
    for j in range(CONV_W):
        off = halo - CONV_LEFT + j
        xc = xc + cw_ref[j:j + 1, :] * ext_ref[off:off + tb, :]

    log_a_unit = -LRU_C * jax.nn.softplus(-lam_ref[...])

    for n in range(N_RNN_BLOCKS):
        cols = slice(n * RNN_BLOCK_W, (n + 1) * RNN_BLOCK_W)
        xn = xc[:, cols]
        g = jnp.dot(xn.astype(BF16), w_ref[n], preferred_element_type=F32)
        r = _sigmoid(g[:, :RNN_BLOCK_W] + ba_ref[:, cols])
        gi = _sigmoid(g[:, RNN_BLOCK_W:] + bx_ref[:, cols])
        log_a = r * log_a_unit[:, cols]
        a_ref[:, cols] = jnp.exp(log_a)
        th = jnp.tanh(log_a)
        u_ref[:, cols] = jnp.sqrt(-2.0 * th / (1.0 - th)) * (gi * xn)

    sub = lax.broadcasted_iota(jnp.int32, (SUBLANES, D_RNN), 0)
    n_groups = tb // SUBLANES

    def group(gidx, carry):
        gi_ = (n_groups - 1 - gidx) if reverse else gidx
        r0 = pl.multiple_of(gi_ * SUBLANES, SUBLANES)
        a = a_ref[pl.ds(r0, SUBLANES), :]
        u = u_ref[pl.ds(r0, SUBLANES), :]
        d = 1
        while d < SUBLANES:
            if reverse:
                keep = sub < SUBLANES - d
                shift = SUBLANES - d
            else:
                keep = sub >= d
                shift = d
            u = jnp.where(keep, u + a * pltpu.roll(u, shift, 0), u)
            a = jnp.where(keep, a * pltpu.roll(a, shift, 0), a)
            d *= 2
        h = u + a * carry
        h_ref[pl.ds(r0, SUBLANES), :] = h
        edge = 0 if reverse else SUBLANES - 1
        return h[edge:edge + 1, :]

    carry_ref[...] = lax.fori_loop(0, n_groups, group, carry_ref[...])


def _rnn_body(xfp_ref, xfc_ref, xfn_ref, xbp_ref, xbc_ref, xbn_ref,
              cw_ref, cb_ref, wf_ref, wb_ref, ba_ref, bx_ref, lam_ref,
              hf_ref, hb_ref, ext_ref, a_ref, u_ref, cf_ref, cbk_ref):
    i = pl.program_id(1)
    last = pl.num_programs(1) - 1

    @pl.when(i == 0)
    def _():
        cf_ref[...] = jnp.zeros_like(cf_ref)
        cbk_ref[...] = jnp.zeros_like(cbk_ref)

    _rnn_direction(False, i == 0, i == last, xfp_ref, xfc_ref, xfn_ref, cw_ref, cb_ref, wf_ref,
                   ba_ref.at[0:1], bx_ref.at[0:1], lam_ref.at[0:1], hf_ref,
                   ext_ref, a_ref, u_ref, cf_ref)
    _rnn_direction(True, i == last, i == 0, xbp_ref, xbc_ref, xbn_ref, cw_ref, cb_ref, wb_ref,
                   ba_ref.at[1:2], bx_ref.at[1:2], lam_ref.at[1:2], hb_ref,
                   ext_ref, a_ref, u_ref, cbk_ref)


def _rnn(x_rnn, conv_w, conv_b, w_fwd, w_bwd, ba, bx, lam):
    batch, seq = x_rnn.shape[0], x_rnn.shape[1]
    tb = TB_RNN
    nb = seq // tb
    per = tb // SUBLANES
    n_halo = seq // SUBLANES

    def fwd_cur(b, i):
        return (b, i, 0)

    def fwd_prev(b, i):
        return (b, jnp.maximum(i * per - 1, 0), 0)

    def fwd_next(b, i):
        return (b, jnp.minimum((i + 1) * per, n_halo - 1), 0)

    def bwd_cur(b, i):
        return (b, nb - 1 - i, 0)

    def bwd_prev(b, i):
        return (b, jnp.maximum((nb - 1 - i) * per - 1, 0), 0)

    def bwd_next(b, i):
        return (b, jnp.minimum((nb - i) * per, n_halo - 1), 0)

    def const(shape):
        return pl.BlockSpec(shape, lambda b, i: (0,) * len(shape))

    tile = (None, tb, D_RNN)
    halo = (None, SUBLANES, D_RNN)
    w_shape = (N_RNN_BLOCKS, RNN_BLOCK_W, 2 * RNN_BLOCK_W)
    out_sds = jax.ShapeDtypeStruct((batch, seq, D_RNN), F32)
    return pl.pallas_call(
        _rnn_body,
        grid=(batch, nb),
        in_specs=[
            pl.BlockSpec(halo, fwd_prev), pl.BlockSpec(tile, fwd_cur), pl.BlockSpec(halo, fwd_next),
            pl.BlockSpec(halo, bwd_prev), pl.BlockSpec(tile, bwd_cur), pl.BlockSpec(halo, bwd_next),
            const((CONV_W, D_RNN)), const((1, D_RNN)), const(w_shape), const(w_shape),
            const((2, D_RNN)), const((2, D_RNN)), const((2, D_RNN)),
        ],
        out_specs=(pl.BlockSpec(tile, fwd_cur), pl.BlockSpec(tile, bwd_cur)),
        out_shape=(out_sds, out_sds),
        scratch_shapes=[
            pltpu.VMEM((tb + 2 * SUBLANES, D_RNN), F32),
            pltpu.VMEM((tb, D_RNN), F32),
            pltpu.VMEM((tb, D_RNN), F32),
            pltpu.VMEM((1, D_RNN), F32),
            pltpu.VMEM((1, D_RNN), F32),
        ],
        compiler_params=pltpu.CompilerParams(
            dimension_semantics=("arbitrary", "arbitrary"), vmem_limit_bytes=VMEM_LIMIT_BYTES),
        name="rglru_scan",
    )(x_rnn, x_rnn, x_rnn, x_rnn, x_rnn, x_rnn, conv_w, conv_b, w_fwd, w_bwd, ba, bx, lam)


def _tail_body(alpha, x_ref, attn_ref, hf_ref, hb_ref, gr_ref, ma_ref, mr_ref,
               wa_ref, wr_ref, wo_ref, bo_ref, lg_ref, lb_ref, y_ref):
    branch_a = jnp.dot(attn_ref[...], wa_ref[...], preferred_element_type=F32)
    rnn = ((hf_ref[...] + hb_ref[...]) * gr_ref[...]).astype(BF16)
    branch_r = jnp.dot(rnn, wr_ref[...], preferred_element_type=F32)
    merged = ma_ref[...] * branch_a + mr_ref[...] * branch_r
    out = jnp.dot(merged.astype(BF16), wo_ref[...], preferred_element_type=F32) + bo_ref[...]
    z = alpha * x_ref[...] + out
    mu = jnp.mean(z, axis=-1, keepdims=True)
    zc = z - mu
    var = jnp.mean(zc * zc, axis=-1, keepdims=True)
    y_ref[...] = zc * lax.rsqrt(var + LN_EPS) * lg_ref[...] + lb_ref[...]


def _tail(alpha, x, attn, hf, hb, gr, ma, mr, wa, wr, wo, bo, lg, lb):
    tokens = x.shape[0]
    tm = TM_TAIL

    def rows(n):
        return pl.BlockSpec((tm, n), lambda i: (i, 0))

    def const(shape):
        return pl.BlockSpec(shape, lambda i: (0, 0))

    return pl.pallas_call(
        functools.partial(_tail_body, alpha),
        grid=(tokens // tm,),
        in_specs=[rows(D_MODEL), rows(D_ATTN), rows(D_RNN), rows(D_RNN), rows(D_RNN),
                  rows(D_MODEL), rows(D_MODEL),
                  const((D_ATTN, D_MODEL)), const((D_RNN, D_MODEL)), const((D_MODEL, D_MODEL)),
                  const((1, D_MODEL)), const((1, D_MODEL)), const((1, D_MODEL))],
        out_specs=rows(D_MODEL),
        out_shape=jax.ShapeDtypeStruct((tokens, D_MODEL), F32),
        compiler_params=pltpu.CompilerParams(
            dimension_semantics=("arbitrary",), vmem_limit_bytes=VMEM_LIMIT_BYTES),
        name="merge_out_norm",
    )(x, attn, hf, hb, gr, ma, mr, wa, wr, wo, bo, lg, lb)


def _rope_tables(seq):
    half = HEAD_DIM // 2
    inv = ROPE_THETA ** (-jnp.arange(half, dtype=F32) * (2.0 / HEAD_DIM))
    ang = jnp.arange(seq).astype(F32)[:, None] * inv[None, :]
    cos, sin = jnp.cos(ang), jnp.sin(ang)
    return jnp.concatenate([cos, cos], axis=-1), jnp.concatenate([-sin, sin], axis=-1)


def kernel(x, w_in, b_in, attn_sink, conv_w, conv_b, lru_wa, lru_ba, lru_wx, lru_bx, lru_lambda,
           w_branch_attn, w_branch_rnn, w_out, b_out, ln_gain, ln_bias):
    batch, seq, _ = x.shape
    depth = w_in.shape[0]
    alpha = (2.0 * depth) ** 0.25
    tokens = batch * seq
    cos_t, sin_t = _rope_tables(seq)
    x2 = x.reshape(tokens, D_MODEL)
    for l in range(depth):
        q, k, v, ga, xr, gr, ma, mr = _inproj(
            x2.astype(BF16), w_in[l].astype(BF16), b_in[l][None, :], cos_t, sin_t, seq)
        attn = _attention(
            attn_sink[l],
            q.reshape(batch, seq, D_ATTN), k.reshape(batch, seq, D_KV),
            v.reshape(batch, seq, D_KV), ga.reshape(batch, seq, D_ATTN))
        w_gates = jnp.concatenate([lru_wa[l], lru_wx[l]], axis=-1).astype(BF16)
        hf, hb = _rnn(xr.reshape(batch, seq, D_RNN), conv_w[l], conv_b[l][None, :],
                      w_gates[0], w_gates[1], lru_ba[l], lru_bx[l], lru_lambda[l])
        x2 = _tail(alpha, x2, attn.reshape(tokens, D_ATTN),
                   hf.reshape(tokens, D_RNN), hb.reshape(tokens, D_RNN), gr, ma, mr,
                   w_branch_attn[l].astype(BF16), w_branch_rnn[l].astype(BF16),
                   w_out[l].astype(BF16), b_out[l][None, :], ln_gain[l][None, :],
                   ln_bias[l][None, :])
    return x2.reshape(batch, seq, D_MODEL)
```

```python
import functools

import jax
import jax.numpy as jnp
import numpy as np
from jax import lax
from jax.experimental import pallas as pl
from jax.experimental.pallas import tpu as pltpu

D_MODEL = 1024
HEAD_DIM = 128
N_Q_HEADS = D_MODEL // HEAD_DIM
N_KV_HEADS = 2
Q_PER_KV = N_Q_HEADS // N_KV_HEADS
D_ATTN = N_Q_HEADS * HEAD_DIM
D_KV = N_KV_HEADS * HEAD_DIM
WINDOW = 128
BLOCK = 128
ROPE_THETA = 10000.0
D_RNN = (5 * D_MODEL) // 4
RNN_BLOCK_W = 128
N_RNN_BLOCKS = D_RNN // RNN_BLOCK_W
CONV_W = 4
CONV_LEFT = 2
LRU_C = 8.0
LN_EPS = 1e-5

LANES = 128
SUBLANES = 8
MXU_WIDTH = 256
VMEM_LIMIT_BYTES = 56 * 1024 * 1024

F32 = jnp.float32
BF16 = jnp.bfloat16

SECTION_SIZES = (D_ATTN, D_KV, D_KV, D_ATTN, D_RNN, D_RNN, D_MODEL, D_MODEL)
SECTION_KINDS = ("rope_q", "rope", "none", "silu", "none", "silu", "sigmoid", "sigmoid")
SECTION_DTYPES = (BF16, BF16, BF16, BF16, F32, BF16, BF16, BF16)

TM_INPROJ = 512
TQ_ATTN = 512
TB_RNN = 256
TM_TAIL = 256


_sigmoid = jax.nn.sigmoid


def _inproj_body(seq_blocks, x_ref, w_ref, b_ref, base_ref, off_ref, *out_refs):
    xb = x_ref[...].astype(BF16)
    tm = x_ref.shape[0]
    blk0 = (pl.program_id(0) * (tm // BLOCK)) % seq_blocks
    cos_rows, sin_rows = [], []
    for jb in range(tm // BLOCK):
        cb = base_ref[0, pl.ds(blk0 + jb, 1), :]
        sb = base_ref[1, pl.ds(blk0 + jb, 1), :]
        cos_rows.append(cb * off_ref[0] - sb * off_ref[1])
        sin_rows.append(sb * off_ref[2] + cb * off_ref[3])
    cos = jnp.concatenate(cos_rows, axis=0)
    sin = jnp.concatenate(sin_rows, axis=0)

    def rope(t):
        return t * cos + pltpu.roll(t, HEAD_DIM // 2, 1) * sin

    start = 0
    for size, kind, out_ref in zip(SECTION_SIZES, SECTION_KINDS, out_refs):
        for c in range(0, size, MXU_WIDTH):
            col = start + c
            t = jnp.dot(xb, w_ref[:, col:col + MXU_WIDTH], preferred_element_type=F32)
            t = t + b_ref[:, col:col + MXU_WIDTH]
            if kind in ("rope", "rope_q"):
                scale = HEAD_DIM ** -0.5 if kind == "rope_q" else 1.0
                for h in range(MXU_WIDTH // HEAD_DIM):
                    th = rope(t[:, h * HEAD_DIM:(h + 1) * HEAD_DIM]) * scale
                    out_ref[:, c + h * HEAD_DIM:c + (h + 1) * HEAD_DIM] = th.astype(out_ref.dtype)
                continue
            if kind == "silu":
                t = t * _sigmoid(t)
            elif kind == "sigmoid":
                t = _sigmoid(t)
            out_ref[:, c:c + MXU_WIDTH] = t.astype(out_ref.dtype)
        start += size


def _inproj(x, w, b, rope_base, rope_off, seq):
    tokens = x.shape[0]
    total = w.shape[1]
    tm = TM_INPROJ
    seq_blocks = seq // BLOCK
    out_shape = tuple(jax.ShapeDtypeStruct((tokens, n), dt)
                      for n, dt in zip(SECTION_SIZES, SECTION_DTYPES))
    out_specs = tuple(pl.BlockSpec((tm, n), lambda i: (i, 0)) for n in SECTION_SIZES)
    return pl.pallas_call(
        functools.partial(_inproj_body, seq_blocks),
        grid=(tokens // tm,),
        in_specs=[
            pl.BlockSpec((tm, D_MODEL), lambda i: (i, 0)),
            pl.BlockSpec((D_MODEL, total), lambda i: (0, 0), pipeline_mode=pl.Buffered(1)),
            pl.BlockSpec((1, total), lambda i: (0, 0)),
            pl.BlockSpec((2, seq_blocks, HEAD_DIM), lambda i: (0, 0, 0)),
            pl.BlockSpec((4, BLOCK, HEAD_DIM), lambda i: (0, 0, 0)),
        ],
        out_specs=out_specs,
        out_shape=out_shape,
        compiler_params=pltpu.CompilerParams(
            dimension_semantics=("arbitrary",), vmem_limit_bytes=VMEM_LIMIT_BYTES),
        name="inproj",
    )(x, w, b, rope_base, rope_off)


def _attn_body(sink_ref, q_ref, kp_ref, kc_ref, kn_ref, vp_ref, vc_ref, vn_ref, g_ref,
               o_ref, kband_ref, vband_ref):
    i = pl.program_id(1)
    last = pl.num_programs(1) - 1
    tq = q_ref.shape[0]
    nblk = tq // BLOCK
    band_w = 3 * BLOCK

    kband_ref[0:BLOCK, :] = kp_ref[...]
    kband_ref[BLOCK:BLOCK + tq, :] = kc_ref[...]
    kband_ref[BLOCK + tq:, :] = kn_ref[...]
    vband_ref[0:BLOCK, :] = vp_ref[...]
    vband_ref[BLOCK:BLOCK + tq, :] = vc_ref[...]
    vband_ref[BLOCK + tq:, :] = vn_ref[...]

    row = lax.broadcasted_iota(jnp.int32, (BLOCK, band_w), 0)
    col = lax.broadcasted_iota(jnp.int32, (BLOCK, band_w), 1)
    in_band = (col >= row + (BLOCK - WINDOW)) & (col <= row + (BLOCK + WINDOW))

    for j in range(nblk):
        valid = in_band
        if j == 0:
            valid = valid & ((col >= BLOCK) | (i > 0))
        if j == nblk - 1:
            valid = valid & ((col < 2 * BLOCK) | (i < last))
        rows = slice(j * BLOCK, (j + 1) * BLOCK)
        for h in range(N_KV_HEADS):
            kj = kband_ref[j * BLOCK:j * BLOCK + band_w, h * HEAD_DIM:(h + 1) * HEAD_DIM]
            vj = vband_ref[j * BLOCK:j * BLOCK + band_w, h * HEAD_DIM:(h + 1) * HEAD_DIM]
            heads = [h * Q_PER_KV + g for g in range(Q_PER_KV)]
            q4 = jnp.concatenate(
                [q_ref[rows, hq * HEAD_DIM:(hq + 1) * HEAD_DIM] for hq in heads], axis=0)
            s4 = lax.dot_general(q4, kj, (((1,), (1,)), ((), ())),
                                 preferred_element_type=F32)
            ps, inv = [], []
            for g, hq in enumerate(heads):
                s = jnp.where(valid, s4[g * BLOCK:(g + 1) * BLOCK], -jnp.inf)
                sk = sink_ref[hq]
                m = jnp.maximum(jnp.max(s, axis=-1, keepdims=True), sk)
                p = jnp.exp(s - m)
                denom = jnp.sum(p, axis=-1, keepdims=True) + jnp.exp(sk - m)
                ps.append(p.astype(BF16))
                inv.append(1.0 / denom)
            o4 = jnp.dot(jnp.concatenate(ps, axis=0), vj, preferred_element_type=F32)
            for g, hq in enumerate(heads):
                cols = slice(hq * HEAD_DIM, (hq + 1) * HEAD_DIM)
                o = o4[g * BLOCK:(g + 1) * BLOCK] * inv[g]
                o_ref[rows, cols] = (o * g_ref[rows, cols].astype(F32)).astype(o_ref.dtype)


def _attention(sink, q, k, v, gate):
    batch, seq = q.shape[0], q.shape[1]
    tq = TQ_ATTN
    per = tq // BLOCK
    n_halo = seq // BLOCK

    def cur(b, i):
        return (b, i, 0)

    def prev(b, i):
        return (b, jnp.maximum(i * per - 1, 0), 0)

    def nxt(b, i):
        return (b, jnp.minimum((i + 1) * per, n_halo - 1), 0)

    kv_cur = pl.BlockSpec((None, tq, D_KV), cur)
    kv_prev = pl.BlockSpec((None, BLOCK, D_KV), prev)
    kv_next = pl.BlockSpec((None, BLOCK, D_KV), nxt)
    return pl.pallas_call(
        _attn_body,
        grid=(batch, seq // tq),
        in_specs=[
            pl.BlockSpec(memory_space=pltpu.SMEM),
            pl.BlockSpec((None, tq, D_ATTN), cur),
            kv_prev, kv_cur, kv_next,
            kv_prev, kv_cur, kv_next,
            pl.BlockSpec((None, tq, D_ATTN), cur),
        ],
        out_specs=pl.BlockSpec((None, tq, D_ATTN), cur),
        out_shape=jax.ShapeDtypeStruct((batch, seq, D_ATTN), BF16),
        scratch_shapes=[pltpu.VMEM((tq + 2 * BLOCK, D_KV), BF16),
                        pltpu.VMEM((tq + 2 * BLOCK, D_KV), BF16)],
        compiler_params=pltpu.CompilerParams(
            dimension_semantics=("arbitrary", "arbitrary"), vmem_limit_bytes=VMEM_LIMIT_BYTES),
        name="band_attention",
    )(sink, q, k, k, k, v, v, v, gate)


def _rnn_direction(reverse, first, last, xp_ref, xc_ref, xn_ref, cw_ref, cb_ref, w_ref,
                   ba_ref, bx_ref, lam_ref, h_ref, ext_ref, a_ref, u_ref, carry_ref):
    tb = xc_ref.shape[0]
    halo = SUBLANES
    ext_ref[0:halo, :] = jnp.where(first, 0.0, xp_ref[...])
    ext_ref[halo:halo + tb, :] = xc_ref[...]
    ext_ref[halo + tb:, :] = jnp.where(last, 0.0, xn_ref[...])

    xc = cb_ref[...]
    for j in range(CONV_W):
        off = halo - CONV_LEFT + j
        xc = xc + cw_ref[j:j + 1, :] * ext_ref[off:off + tb, :]

    log_a_unit = -LRU_C * jax.nn.softplus(-lam_ref[...])

    for n in range(N_RNN_BLOCKS):
        cols = slice(n * RNN_BLOCK_W, (n + 1) * RNN_BLOCK_W)
        xn = xc[:, cols]
        g = jnp.dot(xn.astype(BF16), w_ref[n], preferred_element_type=F32)
        r = _sigmoid(g[:, :RNN_BLOCK_W] + ba_ref[:, cols])
        gi = _sigmoid(g[:, RNN_BLOCK_W:] + bx_ref[:, cols])
        log_a = r * log_a_unit[:, cols]
        a_ref[:, cols] = jnp.exp(log_a)
        th = jnp.tanh(log_a)
        u_ref[:, cols] = jnp.sqrt(-2.0 * th / (1.0 - th)) * (gi * xn)

    sub = lax.broadcasted_iota(jnp.int32, (SUBLANES, D_RNN), 0)
    n_groups = tb // SUBLANES

    def group(gidx, carry):
        gi_ = (n_groups - 1 - gidx) if reverse else gidx
        r0 = pl.multiple_of(gi_ * SUBLANES, SUBLANES)
        a = a_ref[pl.ds(r0, SUBLANES), :]
        u = u_ref[pl.ds(r0, SUBLANES), :]
        d = 1
        while d < SUBLANES:
            if reverse:
                keep = sub < SUBLANES - d
                shift = SUBLANES - d
            else:
                keep = sub >= d
                shift = d
            u = jnp.where(keep, u + a * pltpu.roll(u, shift, 0), u)
            a = jnp.where(keep, a * pltpu.roll(a, shift, 0), a)
            d *= 2
        h = u + a * carry
        h_ref[pl.ds(r0, SUBLANES), :] = h
        edge = 0 if reverse else SUBLANES - 1
        return h[edge:edge + 1, :]

    carry_ref[...] = lax.fori_loop(0, n_groups, group, carry_ref[...])


def _rnn_body(xfp_ref, xfc_ref, xfn_ref, xbp_ref, xbc_ref, xbn_ref,
              cw_ref, cb_ref, wf_ref, wb_ref, ba_ref, bx_ref, lam_ref,
              hf_ref, hb_ref, ext_ref, a_ref, u_ref, cf_ref, cbk_ref):
    i = pl.program_id(1)
    last = pl.num_programs(1) - 1

    @pl.when(i == 0)
    def _():
        cf_ref[...] = jnp.zeros_like(cf_ref)
        cbk_ref[...] = jnp.zeros_like(cbk_ref)

    _rnn_direction(False, i == 0, i == last, xfp_ref, xfc_ref, xfn_ref, cw_ref, cb_ref, wf_ref,
                   ba_ref.at[0:1], bx_ref.at[0:1], lam_ref.at[0:1], hf_ref,
                   ext_ref, a_ref, u_ref, cf_ref)
    _rnn_direction(True, i == last, i == 0, xbp_ref, xbc_ref, xbn_ref, cw_ref, cb_ref, wb_ref,
                   ba_ref.at[1:2], bx_ref.at[1:2], lam_ref.at[1:2], hb_ref,
                   ext_ref, a_ref, u_ref, cbk_ref)


def _rnn(x_rnn, conv_w, conv_b, w_fwd, w_bwd, ba, bx, lam):
    batch, seq = x_rnn.shape[0], x_rnn.shape[1]
    tb = TB_RNN
    nb = seq // tb
    per = tb // SUBLANES
    n_halo = seq // SUBLANES

    def fwd_cur(b, i):
        return (b, i, 0)

    def fwd_prev(b, i):
        return (b, jnp.maximum(i * per - 1, 0), 0)

    def fwd_next(b, i):
        return (b, jnp.minimum((i + 1) * per, n_halo - 1), 0)

    def bwd_cur(b, i):
        return (b, nb - 1 - i, 0)

    def bwd_prev(b, i):
        return (b, jnp.maximum((nb - 1 - i) * per - 1, 0), 0)

    def bwd_next(b, i):
        return (b, jnp.minimum((nb - i) * per, n_halo - 1), 0)

    def const(shape):
        return pl.BlockSpec(shape, lambda b, i: (0,) * len(shape))

    tile = (None, tb, D_RNN)
    halo = (None, SUBLANES, D_RNN)
    w_shape = (N_RNN_BLOCKS, RNN_BLOCK_W, 2 * RNN_BLOCK_W)
    out_sds = jax.ShapeDtypeStruct((batch, seq, D_RNN), F32)
    return pl.pallas_call(
        _rnn_body,
        grid=(batch, nb),
        in_specs=[
            pl.BlockSpec(halo, fwd_prev), pl.BlockSpec(tile, fwd_cur), pl.BlockSpec(halo, fwd_next),
            pl.BlockSpec(halo, bwd_prev), pl.BlockSpec(tile, bwd_cur), pl.BlockSpec(halo, bwd_next),
            const((CONV_W, D_RNN)), const((1, D_RNN)), const(w_shape), const(w_shape),
            const((2, D_RNN)), const((2, D_RNN)), const((2, D_RNN)),
        ],
        out_specs=(pl.BlockSpec(tile, fwd_cur), pl.BlockSpec(tile, bwd_cur)),
        out_shape=(out_sds, out_sds),
        scratch_shapes=[
            pltpu.VMEM((tb + 2 * SUBLANES, D_RNN), F32),
            pltpu.VMEM((tb, D_RNN), F32),
            pltpu.VMEM((tb, D_RNN), F32),
            pltpu.VMEM((1, D_RNN), F32),
            pltpu.VMEM((1, D_RNN), F32),
        ],
        compiler_params=pltpu.CompilerParams(
            dimension_semantics=("arbitrary", "arbitrary"), vmem_limit_bytes=VMEM_LIMIT_BYTES),
        name="rglru_scan",
    )(x_rnn, x_rnn, x_rnn, x_rnn, x_rnn, x_rnn, conv_w, conv_b, w_fwd, w_bwd, ba, bx, lam)


def _tail_body(alpha, x_ref, attn_ref, hf_ref, hb_ref, gr_ref, ma_ref, mr_ref,
               wa_ref, wr_ref, wo_ref, bo_ref, lg_ref, lb_ref, y_ref):
    branch_a = jnp.dot(attn_ref[...], wa_ref[...], preferred_element_type=F32)
    rnn = ((hf_ref[...] + hb_ref[...]) * gr_ref[...].astype(F32)).astype(BF16)
    branch_r = jnp.dot(rnn, wr_ref[...], preferred_element_type=F32)
    merged = ma_ref[...].astype(F32) * branch_a + mr_ref[...].astype(F32) * branch_r
    out = jnp.dot(merged.astype(BF16), wo_ref[...], preferred_element_type=F32) + bo_ref[...]
    z = alpha * x_ref[...] + out
    mu = jnp.mean(z, axis=-1, keepdims=True)
    zc = z - mu
    var = jnp.mean(zc * zc, axis=-1, keepdims=True)
    y_ref[...] = zc * lax.rsqrt(var + LN_EPS) * lg_ref[...] + lb_ref[...]


def _tail(alpha, x, attn, hf, hb, gr, ma, mr, wa, wr, wo, bo, lg, lb):
    tokens = x.shape[0]
    tm = TM_TAIL

    def rows(n):
        return pl.BlockSpec((tm, n), lambda i: (i, 0))

    def const(shape):
        return pl.BlockSpec(shape, lambda i: (0, 0))

    return pl.pallas_call(
        functools.partial(_tail_body, alpha),
        grid=(tokens // tm,),
        in_specs=[rows(D_MODEL), rows(D_ATTN), rows(D_RNN), rows(D_RNN), rows(D_RNN),
                  rows(D_MODEL), rows(D_MODEL),
                  const((D_ATTN, D_MODEL)), const((D_RNN, D_MODEL)), const((D_MODEL, D_MODEL)),
                  const((1, D_MODEL)), const((1, D_MODEL)), const((1, D_MODEL))],
        out_specs=rows(D_MODEL),
        out_shape=jax.ShapeDtypeStruct((tokens, D_MODEL), F32),
        compiler_params=pltpu.CompilerParams(
            dimension_semantics=("arbitrary",), vmem_limit_bytes=VMEM_LIMIT_BYTES),
        name="merge_out_norm",
    )(x, attn, hf, hb, gr, ma, mr, wa, wr, wo, bo, lg, lb)


def _rope_tables(seq):
    half = HEAD_DIM // 2
    inv = ROPE_THETA ** (-np.arange(half, dtype=np.float64) * (2.0 / HEAD_DIM))
    dup = lambda t: np.concatenate([t, t], axis=-1)
    sign = np.concatenate([-np.ones(half), np.ones(half)])
    base_ang = (BLOCK * np.arange(seq // BLOCK, dtype=np.float64))[:, None] * inv[None, :]
    off_ang = np.arange(BLOCK, dtype=np.float64)[:, None] * inv[None, :]
    base = np.stack([dup(np.cos(base_ang)), dup(np.sin(base_ang))])
    co, so = dup(np.cos(off_ang)), dup(np.sin(off_ang))
    off = np.stack([co, so, sign * co, sign * so])
    return jnp.asarray(base, F32), jnp.asarray(off, F32)


def kernel(x, w_in, b_in, attn_sink, conv_w, conv_b, lru_wa, lru_ba, lru_wx, lru_bx, lru_lambda,
           w_branch_attn, w_branch_rnn, w_out, b_out, ln_gain, ln_bias):
    batch, seq, _ = x.shape
    depth = w_in.shape[0]
    alpha = (2.0 * depth) ** 0.25
    tokens = batch * seq
    rope_base, rope_off = _rope_tables(seq)
    x2 = x.reshape(tokens, D_MODEL)
    for l in range(depth):
        q, k, v, ga, xr, gr, ma, mr = _inproj(
            x2, w_in[l].astype(BF16), b_in[l][None, :], rope_base, rope_off, seq)
        attn = _attention(
            attn_sink[l],
            q.reshape(batch, seq, D_ATTN), k.reshape(batch, seq, D_KV),
            v.reshape(batch, seq, D_KV), ga.reshape(batch, seq, D_ATTN))
        w_gates = jnp.concatenate([lru_wa[l], lru_wx[l]], axis=-1).astype(BF16)
        hf, hb = _rnn(xr.reshape(batch, seq, D_RNN), conv_w[l], conv_b[l][None, :],
                      w_gates[0], w_gates[1], lru_ba[l], lru_bx[l], lru_lambda[l])
        x2 = _tail(alpha, x2, attn.reshape(tokens, D_ATTN),
                   hf.reshape(tokens, D_RNN), hb.reshape(tokens, D_RNN), gr, ma, mr,
                   w_branch_attn[l].astype(BF16), w_branch_rnn[l].astype(BF16),
                   w_out[l].astype(BF16), b_out[l][None, :], ln_gain[l][None, :],
                   ln_bias[l][None, :])
    return x2.reshape(batch, seq, D_MODEL)
```

```python
import functools

import jax
import jax.numpy as jnp
import numpy as np
from jax import lax
from jax.experimental import pallas as pl
from jax.experimental.pallas import tpu as pltpu

D_MODEL = 1024
HEAD_DIM = 128
N_Q_HEADS = D_MODEL // HEAD_DIM
N_KV_HEADS = 2
Q_PER_KV = N_Q_HEADS // N_KV_HEADS
D_ATTN = N_Q_HEADS * HEAD_DIM
D_KV = N_KV_HEADS * HEAD_DIM
WINDOW = 128
BLOCK = 128
ROPE_THETA = 10000.0
D_RNN = (5 * D_MODEL) // 4
RNN_BLOCK_W = 128
N_RNN_BLOCKS = D_RNN // RNN_BLOCK_W
CONV_W = 4
CONV_LEFT = 2
CONV_RIGHT = CONV_W - 1 - CONV_LEFT
LRU_C = 8.0
LN_EPS = 1e-5

LANES = 128
SUBLANES = 8
BF16_SUBLANES = 16
MXU_WIDTH = 256
VMEM_LIMIT_BYTES = 56 * 1024 * 1024

F32 = jnp.float32
BF16 = jnp.bfloat16

SECTION_SIZES = (D_ATTN, D_KV, D_KV, D_ATTN, D_RNN, D_RNN, D_MODEL, D_MODEL)
SECTION_KINDS = ("rope_q", "rope", "none", "silu", "none", "silu", "sigmoid", "sigmoid")
SECTION_DTYPES = (BF16, BF16, BF16, BF16, F32, BF16, BF16, BF16)

TM_INPROJ = 512
TQ_ATTN = 512
TB_RNN = 256
TM_TAIL = 256

assert RNN_BLOCK_W == LANES


_sigmoid = jax.nn.sigmoid


def _inproj_body(seq_blocks, x_ref, w_ref, b_ref, base_ref, off_ref, *out_refs):
    xb = x_ref[...].astype(BF16)
    tm = x_ref.shape[0]
    blk0 = (pl.program_id(0) * (tm // BLOCK)) % seq_blocks
    cos_rows, sin_rows = [], []
    for jb in range(tm // BLOCK):
        cb = base_ref[0, pl.ds(blk0 + jb, 1), :]
        sb = base_ref[1, pl.ds(blk0 + jb, 1), :]
        cos_rows.append(cb * off_ref[0] - sb * off_ref[1])
        sin_rows.append(sb * off_ref[2] + cb * off_ref[3])
    cos = jnp.concatenate(cos_rows, axis=0)
    sin = jnp.concatenate(sin_rows, axis=0)

    def rope(t):
        return t * cos + pltpu.roll(t, HEAD_DIM // 2, 1) * sin

    start = 0
    for size, kind, out_ref in zip(SECTION_SIZES, SECTION_KINDS, out_refs):
        for c in range(0, size, MXU_WIDTH):
            col = start + c
            t = jnp.dot(xb, w_ref[:, col:col + MXU_WIDTH], preferred_element_type=F32)
            t = t + b_ref[:, col:col + MXU_WIDTH]
            if kind in ("rope", "rope_q"):
                scale = HEAD_DIM ** -0.5 if kind == "rope_q" else 1.0
                for h in range(MXU_WIDTH // HEAD_DIM):
                    th = rope(t[:, h * HEAD_DIM:(h + 1) * HEAD_DIM]) * scale
                    out_ref[:, c + h * HEAD_DIM:c + (h + 1) * HEAD_DIM] = th.astype(out_ref.dtype)
                continue
            if kind == "silu":
                t = t * _sigmoid(t)
            elif kind == "sigmoid":
                t = _sigmoid(t)
            out_ref[:, c:c + MXU_WIDTH] = t.astype(out_ref.dtype)
        start += size


def _inproj(x, w, b, rope_base, rope_off, seq):
    tokens = x.shape[0]
    total = w.shape[1]
    tm = TM_INPROJ
    seq_blocks = seq // BLOCK
    out_shape = tuple(jax.ShapeDtypeStruct((tokens, n), dt)
                      for n, dt in zip(SECTION_SIZES, SECTION_DTYPES))
    out_specs = tuple(pl.BlockSpec((tm, n), lambda i: (i, 0)) for n in SECTION_SIZES)
    return pl.pallas_call(
        functools.partial(_inproj_body, seq_blocks),
        grid=(tokens // tm,),
        in_specs=[
            pl.BlockSpec((tm, D_MODEL), lambda i: (i, 0)),
            pl.BlockSpec((D_MODEL, total), lambda i: (0, 0), pipeline_mode=pl.Buffered(1)),
            pl.BlockSpec((1, total), lambda i: (0, 0)),
            pl.BlockSpec((2, seq_blocks, HEAD_DIM), lambda i: (0, 0, 0)),
            pl.BlockSpec((4, BLOCK, HEAD_DIM), lambda i: (0, 0, 0)),
        ],
        out_specs=out_specs,
        out_shape=out_shape,
        compiler_params=pltpu.CompilerParams(
            dimension_semantics=("arbitrary",), vmem_limit_bytes=VMEM_LIMIT_BYTES),
        name="inproj",
    )(x, w, b, rope_base, rope_off)


def _attn_body(sink_ref, q_ref, kp_ref, kc_ref, kn_ref, vp_ref, vc_ref, vn_ref, g_ref,
               o_ref, kband_ref, vband_ref):
    i = pl.program_id(1)
    last = pl.num_programs(1) - 1
    tq = q_ref.shape[0]
    nblk = tq // BLOCK
    band_w = 3 * BLOCK

    kband_ref[0:BLOCK, :] = kp_ref[...]
    kband_ref[BLOCK:BLOCK + tq, :] = kc_ref[...]
    kband_ref[BLOCK + tq:, :] = kn_ref[...]
    vband_ref[0:BLOCK, :] = vp_ref[...]
    vband_ref[BLOCK:BLOCK + tq, :] = vc_ref[...]
    vband_ref[BLOCK + tq:, :] = vn_ref[...]

    row = lax.broadcasted_iota(jnp.int32, (BLOCK, band_w), 0)
    col = lax.broadcasted_iota(jnp.int32, (BLOCK, band_w), 1)
    in_band = (col >= row + (BLOCK - WINDOW)) & (col <= row + (BLOCK + WINDOW))

    for j in range(nblk):
        valid = in_band
        if j == 0:
            valid = valid & ((col >= BLOCK) | (i > 0))
        if j == nblk - 1:
            valid = valid & ((col < 2 * BLOCK) | (i < last))
        rows = slice(j * BLOCK, (j + 1) * BLOCK)
        for h in range(N_KV_HEADS):
            kj = kband_ref[j * BLOCK:j * BLOCK + band_w, h * HEAD_DIM:(h + 1) * HEAD_DIM]
            vj = vband_ref[j * BLOCK:j * BLOCK + band_w, h * HEAD_DIM:(h + 1) * HEAD_DIM]
            heads = [h * Q_PER_KV + g for g in range(Q_PER_KV)]
            q4 = jnp.concatenate(
                [q_ref[rows, hq * HEAD_DIM:(hq + 1) * HEAD_DIM] for hq in heads], axis=0)
            s4 = lax.dot_general(q4, kj, (((1,), (1,)), ((), ())),
                                 preferred_element_type=F32)
            ps, inv = [], []
            for g, hq in enumerate(heads):
                s = jnp.where(valid, s4[g * BLOCK:(g + 1) * BLOCK], -jnp.inf)
                sk = sink_ref[hq]
                m = jnp.maximum(jnp.max(s, axis=-1, keepdims=True), sk)
                p = jnp.exp(s - m)
                denom = jnp.sum(p, axis=-1, keepdims=True) + jnp.exp(sk - m)
                ps.append(p.astype(BF16))
                inv.append(1.0 / denom)
            o4 = jnp.dot(jnp.concatenate(ps, axis=0), vj, preferred_element_type=F32)
            for g, hq in enumerate(heads):
                cols = slice(hq * HEAD_DIM, (hq + 1) * HEAD_DIM)
                o = o4[g * BLOCK:(g + 1) * BLOCK] * inv[g]
                o_ref[rows, cols] = (o * g_ref[rows, cols].astype(F32)).astype(o_ref.dtype)


def _attention(sink, q, k, v, gate):
    batch, seq = q.shape[0], q.shape[1]
    tq = TQ_ATTN
    per = tq // BLOCK
    n_halo = seq // BLOCK

    def cur(b, i):
        return (b, i, 0)

    def prev(b, i):
        return (b, jnp.maximum(i * per - 1, 0), 0)

    def nxt(b, i):
        return (b, jnp.minimum((i + 1) * per, n_halo - 1), 0)

    kv_cur = pl.BlockSpec((None, tq, D_KV), cur)
    kv_prev = pl.BlockSpec((None, BLOCK, D_KV), prev)
    kv_next = pl.BlockSpec((None, BLOCK, D_KV), nxt)
    return pl.pallas_call(
        _attn_body,
        grid=(batch, seq // tq),
        in_specs=[
            pl.BlockSpec(memory_space=pltpu.SMEM),
            pl.BlockSpec((None, tq, D_ATTN), cur),
            kv_prev, kv_cur, kv_next,
            kv_prev, kv_cur, kv_next,
            pl.BlockSpec((None, tq, D_ATTN), cur),
        ],
        out_specs=pl.BlockSpec((None, tq, D_ATTN), cur),
        out_shape=jax.ShapeDtypeStruct((batch, seq, D_ATTN), BF16),
        scratch_shapes=[pltpu.VMEM((tq + 2 * BLOCK, D_KV), BF16),
                        pltpu.VMEM((tq + 2 * BLOCK, D_KV), BF16)],
        compiler_params=pltpu.CompilerParams(
            dimension_semantics=("arbitrary", "arbitrary"), vmem_limit_bytes=VMEM_LIMIT_BYTES),
        name="band_attention",
    )(sink, q, k, k, k, v, v, v, gate)


def _rnn_direction(reverse, first, last, xp_ref, xc_ref, xn_ref, cw_ref, cb_ref, w_ref,
                   ba_ref, bx_ref, lam_ref, h_ref, ext_ref, a_ref, u_ref, hp_ref, carry_ref):
    tb = xc_ref.shape[0]
    sub_len = tb // SUBLANES
    sub = lax.broadcasted_iota(jnp.int32, (SUBLANES, LANES), 0)
    half_rate = (0.5 * LRU_C) * jax.nn.softplus(-lam_ref[...])
    zero_row = jnp.zeros((1, LANES), F32)

    for n in range(N_RNN_BLOCKS):
        cols = slice(n * LANES, (n + 1) * LANES)
        ext_n = ext_ref.at[n]
        for s in range(SUBLANES):
            for k in range(sub_len // SUBLANES):
                t0 = k * SUBLANES
                v = xc_ref[s * sub_len + t0:s * sub_len + t0 + SUBLANES, cols]
                ext_n[pl.ds((t0 + CONV_LEFT) * SUBLANES + s, SUBLANES, stride=SUBLANES), :] = v
        for j in range(CONV_LEFT):
            before = jnp.where(first, zero_row, xp_ref[SUBLANES - CONV_LEFT + j:SUBLANES - CONV_LEFT + j + 1, cols])
            src = ext_n[(sub_len + j) * SUBLANES:(sub_len + j + 1) * SUBLANES, :]
            ext_n[j * SUBLANES:(j + 1) * SUBLANES, :] = jnp.where(sub == 0, before, pltpu.roll(src, 1, 0))
        for j in range(CONV_RIGHT):
            after = jnp.where(last, zero_row, xn_ref[j:j + 1, cols])
            src = ext_n[(CONV_LEFT + j) * SUBLANES:(CONV_LEFT + j + 1) * SUBLANES, :]
            e = sub_len + CONV_LEFT + j
            ext_n[e * SUBLANES:(e + 1) * SUBLANES, :] = jnp.where(
                sub == SUBLANES - 1, after, pltpu.roll(src, SUBLANES - 1, 0))

        xcn = cb_ref[:, cols]
        for j in range(CONV_W):
            xcn = xcn + cw_ref[j:j + 1, cols] * ext_n[j * SUBLANES:j * SUBLANES + tb, :]
        g = jnp.dot(xcn.astype(BF16), w_ref[n], preferred_element_type=F32)
        t_r = jnp.tanh(g[:, :LANES] + 0.5 * ba_ref[:, cols])
        t_i = jnp.tanh(g[:, LANES:] + 0.5 * bx_ref[:, cols])
        rate = half_rate[:, cols]
        neg_log_a = rate + rate * t_r
        a_ref[n] = jnp.exp(-neg_log_a)
        m = jnp.tanh(neg_log_a)
        half_root = jnp.where(m > 0.0, m * lax.rsqrt(m * (2.0 + 2.0 * m)), 0.0)
        u_ref[n] = half_root * (xcn + xcn * t_i)

    steps = range(sub_len - 1, -1, -1) if reverse else range(sub_len)

    def step_rows(t):
        return slice(t * SUBLANES, (t + 1) * SUBLANES)

    slab = (N_RNN_BLOCKS, SUBLANES, LANES)
    h = jnp.zeros(slab, F32)
    decay = jnp.ones(slab, F32)
    for t in steps:
        a_t = a_ref[:, step_rows(t), :]
        h = a_t * h + u_ref[:, step_rows(t), :]
        decay = decay * a_t

    sub3 = lax.broadcasted_iota(jnp.int32, slab, 1)
    d = 1
    while d < SUBLANES:
        keep = (sub3 < SUBLANES - d) if reverse else (sub3 >= d)
        shift = (SUBLANES - d) if reverse else d
        h = jnp.where(keep, h + decay * pltpu.roll(h, shift, 1), h)
        decay = jnp.where(keep, decay * pltpu.roll(decay, shift, 1), decay)
        d *= 2
    carry = carry_ref[...]
    h_end = h + decay * carry
    if reverse:
        h = jnp.where(sub3 == SUBLANES - 1, carry, pltpu.roll(h_end, SUBLANES - 1, 1))
        leaving = h_end[:, 0:1, :]
    else:
        h = jnp.where(sub3 == 0, carry, pltpu.roll(h_end, 1, 1))
        leaving = h_end[:, SUBLANES - 1:SUBLANES, :]
    carry_ref[...] = jnp.broadcast_to(leaving, slab)

    for t in steps:
        h = a_ref[:, step_rows(t), :] * h + u_ref[:, step_rows(t), :]
        hp_ref[:, step_rows(t), :] = h

    for n in range(N_RNN_BLOCKS):
        cols = slice(n * LANES, (n + 1) * LANES)
        hp_n = hp_ref.at[n]
        for s in range(SUBLANES):
            for k in range(sub_len // BF16_SUBLANES):
                t0 = k * BF16_SUBLANES
                parts = [hp_n[pl.ds((t0 + half) * SUBLANES + s, SUBLANES, stride=SUBLANES), :]
                         for half in (0, SUBLANES)]
                r0 = s * sub_len + t0
                h_ref[r0:r0 + BF16_SUBLANES, cols] = jnp.concatenate(parts, axis=0).astype(h_ref.dtype)


def _rnn_body(xfp_ref, xfc_ref, xfn_ref, xbp_ref, xbc_ref, xbn_ref,
              cw_ref, cb_ref, wf_ref, wb_ref, ba_ref, bx_ref, lam_ref,
              hf_ref, hb_ref, ext_ref, a_ref, u_ref, hp_ref, cf_ref, cbk_ref):
    i = pl.program_id(1)
    last = pl.num_programs(1) - 1

    @pl.when(i == 0)
    def _():
        cf_ref[...] = jnp.zeros_like(cf_ref)
        cbk_ref[...] = jnp.zeros_like(cbk_ref)

    _rnn_direction(False, i == 0, i == last, xfp_ref, xfc_ref, xfn_ref, cw_ref, cb_ref, wf_ref,
                   ba_ref.at[0:1], bx_ref.at[0:1], lam_ref.at[0:1], hf_ref,
                   ext_ref, a_ref, u_ref, hp_ref, cf_ref)
    _rnn_direction(True, i == last, i == 0, xbp_ref, xbc_ref, xbn_ref, cw_ref, cb_ref, wb_ref,
                   ba_ref.at[1:2], bx_ref.at[1:2], lam_ref.at[1:2], hb_ref,
                   ext_ref, a_ref, u_ref, hp_ref, cbk_ref)


def _rnn(x_rnn, conv_w, conv_b, w_fwd, w_bwd, ba, bx, lam):
    batch, seq = x_rnn.shape[0], x_rnn.shape[1]
    tb = TB_RNN
    nb = seq // tb
    per = tb // SUBLANES
    n_halo = seq // SUBLANES
    assert (tb // SUBLANES) % BF16_SUBLANES == 0

    def fwd_cur(b, i):
        return (b, i, 0)

    def fwd_prev(b, i):
        return (b, jnp.maximum(i * per - 1, 0), 0)

    def fwd_next(b, i):
        return (b, jnp.minimum((i + 1) * per, n_halo - 1), 0)

    def bwd_cur(b, i):
        return (b, nb - 1 - i, 0)

    def bwd_prev(b, i):
        return (b, jnp.maximum((nb - 1 - i) * per - 1, 0), 0)

    def bwd_next(b, i):
        return (b, jnp.minimum((nb - i) * per, n_halo - 1), 0)

    def const(shape):
        return pl.BlockSpec(shape, lambda b, i: (0,) * len(shape))

    tile = (None, tb, D_RNN)
    halo = (None, SUBLANES, D_RNN)
    w_shape = (N_RNN_BLOCKS, RNN_BLOCK_W, 2 * RNN_BLOCK_W)
    out_sds = jax.ShapeDtypeStruct((batch, seq, D_RNN), BF16)
    slab_rows = lambda rows: pltpu.VMEM((N_RNN_BLOCKS, rows, LANES), F32)
    return pl.pallas_call(
        _rnn_body,
        grid=(batch, nb),
        in_specs=[
            pl.BlockSpec(halo, fwd_prev), pl.BlockSpec(tile, fwd_cur), pl.BlockSpec(halo, fwd_next),
            pl.BlockSpec(halo, bwd_prev), pl.BlockSpec(tile, bwd_cur), pl.BlockSpec(halo, bwd_next),
            const((CONV_W, D_RNN)), const((1, D_RNN)), const(w_shape), const(w_shape),
            const((2, D_RNN)), const((2, D_RNN)), const((2, D_RNN)),
        ],
        out_specs=(pl.BlockSpec(tile, fwd_cur), pl.BlockSpec(tile, bwd_cur)),
        out_shape=(out_sds, out_sds),
        scratch_shapes=[
            slab_rows(tb + (CONV_W - 1) * SUBLANES),
            slab_rows(tb),
            slab_rows(tb),
            slab_rows(tb),
            slab_rows(SUBLANES),
            slab_rows(SUBLANES),
        ],
        compiler_params=pltpu.CompilerParams(
            dimension_semantics=("arbitrary", "arbitrary"), vmem_limit_bytes=VMEM_LIMIT_BYTES),
        name="rglru_scan",
    )(x_rnn, x_rnn, x_rnn, x_rnn, x_rnn, x_rnn, conv_w, conv_b, w_fwd, w_bwd, ba, bx, lam)


def _tail_body(alpha, x_ref, attn_ref, hf_ref, hb_ref, gr_ref, ma_ref, mr_ref,
               wa_ref, wr_ref, wo_ref, bo_ref, lg_ref, lb_ref, y_ref):
    branch_a = jnp.dot(attn_ref[...], wa_ref[...], preferred_element_type=F32)
    h = hf_ref[...].astype(F32) + hb_ref[...].astype(F32)
    rnn = (h * gr_ref[...].astype(F32)).astype(BF16)
    branch_r = jnp.dot(rnn, wr_ref[...], preferred_element_type=F32)
    merged = ma_ref[...].astype(F32) * branch_a + mr_ref[...].astype(F32) * branch_r
    out = jnp.dot(merged.astype(BF16), wo_ref[...], preferred_element_type=F32) + bo_ref[...]
    z = alpha * x_ref[...] + out
    mu = jnp.mean(z, axis=-1, keepdims=True)
    zc = z - mu
    var = jnp.mean(zc * zc, axis=-1, keepdims=True)
    y_ref[...] = zc * lax.rsqrt(var + LN_EPS) * lg_ref[...] + lb_ref[...]


def _tail(alpha, x, attn, hf, hb, gr, ma, mr, wa, wr, wo, bo, lg, lb):
    tokens = x.shape[0]
    tm = TM_TAIL

    def rows(n):
        return pl.BlockSpec((tm, n), lambda i: (i, 0))

    def const(shape):
        return pl.BlockSpec(shape, lambda i: (0, 0))

    return pl.pallas_call(
        functools.partial(_tail_body, alpha),
        grid=(tokens // tm,),
        in_specs=[rows(D_MODEL), rows(D_ATTN), rows(D_RNN), rows(D_RNN), rows(D_RNN),
                  rows(D_MODEL), rows(D_MODEL),
                  const((D_ATTN, D_MODEL)), const((D_RNN, D_MODEL)), const((D_MODEL, D_MODEL)),
                  const((1, D_MODEL)), const((1, D_MODEL)), const((1, D_MODEL))],
        out_specs=rows(D_MODEL),
        out_shape=jax.ShapeDtypeStruct((tokens, D_MODEL), F32),
        compiler_params=pltpu.CompilerParams(
            dimension_semantics=("arbitrary",), vmem_limit_bytes=VMEM_LIMIT_BYTES),
        name="merge_out_norm",
    )(x, attn, hf, hb, gr, ma, mr, wa, wr, wo, bo, lg, lb)


def _rope_tables(seq):
    half = HEAD_DIM // 2
    inv = ROPE_THETA ** (-np.arange(half, dtype=np.float64) * (2.0 / HEAD_DIM))
    dup = lambda t: np.concatenate([t, t], axis=-1)
    sign = np.concatenate([-np.ones(half), np.ones(half)])
    base_ang = (BLOCK * np.arange(seq // BLOCK, dtype=np.float64))[:, None] * inv[None, :]
    off_ang = np.arange(BLOCK, dtype=np.float64)[:, None] * inv[None, :]
    base = np.stack([dup(np.cos(base_ang)), dup(np.sin(base_ang))])
    co, so = dup(np.cos(off_ang)), dup(np.sin(off_ang))
    off = np.stack([co, so, sign * co, sign * so])
    return jnp.asarray(base, F32), jnp.asarray(off, F32)


def kernel(x, w_in, b_in, attn_sink, conv_w, conv_b, lru_wa, lru_ba, lru_wx, lru_bx, lru_lambda,
           w_branch_attn, w_branch_rnn, w_out, b_out, ln_gain, ln_bias):
    batch, seq, _ = x.shape
    depth = w_in.shape[0]
    alpha = (2.0 * depth) ** 0.25
    tokens = batch * seq
    rope_base, rope_off = _rope_tables(seq)
    x2 = x.reshape(tokens, D_MODEL)
    for l in range(depth):
        q, k, v, ga, xr, gr, ma, mr = _inproj(
            x2, w_in[l].astype(BF16), b_in[l][None, :], rope_base, rope_off, seq)
        attn = _attention(
            attn_sink[l],
            q.reshape(batch, seq, D_ATTN), k.reshape(batch, seq, D_KV),
            v.reshape(batch, seq, D_KV), ga.reshape(batch, seq, D_ATTN))
        w_gates = (0.5 * jnp.concatenate([lru_wa[l], lru_wx[l]], axis=-1)).astype(BF16)
        hf, hb = _rnn(xr.reshape(batch, seq, D_RNN), conv_w[l], conv_b[l][None, :],
                      w_gates[0], w_gates[1], lru_ba[l], lru_bx[l], lru_lambda[l])
        x2 = _tail(alpha, x2, attn.reshape(tokens, D_ATTN),
                   hf.reshape(tokens, D_RNN), hb.reshape(tokens, D_RNN), gr, ma, mr,
                   w_branch_attn[l].astype(BF16), w_branch_rnn[l].astype(BF16),
                   w_out[l].astype(BF16), b_out[l][None, :], ln_gain[l][None, :],
                   ln_bias[l][None, :])
    return x2.reshape(batch, seq, D_MODEL)
```

```python
import functools

import jax
import jax.numpy as jnp
import numpy as np
from jax import lax
from jax.experimental import pallas as pl
from jax.experimental.pallas import tpu as pltpu

D_MODEL = 1024
HEAD_DIM = 128
N_Q_HEADS = D_MODEL // HEAD_DIM
N_KV_HEADS = 2
Q_PER_KV = N_Q_HEADS // N_KV_HEADS
D_ATTN = N_Q_HEADS * HEAD_DIM
D_KV = N_KV_HEADS * HEAD_DIM
WINDOW = 128
BLOCK = 128
ROPE_THETA = 10000.0
D_RNN = (5 * D_MODEL) // 4
RNN_BLOCK_W = 128
N_RNN_BLOCKS = D_RNN // RNN_BLOCK_W
CONV_W = 4
CONV_LEFT = 2
CONV_RIGHT = CONV_W - 1 - CONV_LEFT
LRU_C = 8.0
LN_EPS = 1e-5

LANES = 128
SUBLANES = 8
BF16_SUBLANES = 16
MXU_WIDTH = 256
VMEM_LIMIT_BYTES = 56 * 1024 * 1024

F32 = jnp.float32
BF16 = jnp.bfloat16

SECTION_SIZES = (D_ATTN, D_KV, D_KV, D_ATTN, D_RNN, D_RNN, D_MODEL, D_MODEL)
SECTION_KINDS = ("rope_q", "rope", "none", "silu", "none", "silu", "sigmoid", "sigmoid")
SECTION_DTYPES = (BF16, BF16, BF16, BF16, F32, BF16, BF16, BF16)

TM_INPROJ = 512
TQ_ATTN = 512
TB_RNN = 256
TM_TAIL = 256

assert RNN_BLOCK_W == LANES


_sigmoid = jax.nn.sigmoid


def _inproj_body(seq_blocks, x_ref, w_ref, b_ref, base_ref, off_ref, *out_refs):
    xb = x_ref[...].astype(BF16)
    tm = x_ref.shape[0]
    blk0 = (pl.program_id(0) * (tm // BLOCK)) % seq_blocks
    cos_rows, sin_rows = [], []
    for jb in range(tm // BLOCK):
        cb = base_ref[0, pl.ds(blk0 + jb, 1), :]
        sb = base_ref[1, pl.ds(blk0 + jb, 1), :]
        cos_rows.append(cb * off_ref[0] - sb * off_ref[1])
        sin_rows.append(sb * off_ref[2] + cb * off_ref[3])
    cos = jnp.concatenate(cos_rows, axis=0)
    sin = jnp.concatenate(sin_rows, axis=0)

    def rope(t):
        return t * cos + pltpu.roll(t, HEAD_DIM // 2, 1) * sin

    start = 0
    for size, kind, out_ref in zip(SECTION_SIZES, SECTION_KINDS, out_refs):
        for c in range(0, size, MXU_WIDTH):
            col = start + c
            t = jnp.dot(xb, w_ref[:, col:col + MXU_WIDTH], preferred_element_type=F32)
            t = t + b_ref[:, col:col + MXU_WIDTH]
            if kind in ("rope", "rope_q"):
                scale = HEAD_DIM ** -0.5 if kind == "rope_q" else 1.0
                for h in range(MXU_WIDTH // HEAD_DIM):
                    th = rope(t[:, h * HEAD_DIM:(h + 1) * HEAD_DIM]) * scale
                    out_ref[:, c + h * HEAD_DIM:c + (h + 1) * HEAD_DIM] = th.astype(out_ref.dtype)
                continue
            if kind == "silu":
                t = t * _sigmoid(t)
            elif kind == "sigmoid":
                t = _sigmoid(t)
            out_ref[:, c:c + MXU_WIDTH] = t.astype(out_ref.dtype)
        start += size


def _inproj(x, w, b, rope_base, rope_off, seq):
    tokens = x.shape[0]
    total = w.shape[1]
    tm = TM_INPROJ
    seq_blocks = seq // BLOCK
    out_shape = tuple(jax.ShapeDtypeStruct((tokens, n), dt)
                      for n, dt in zip(SECTION_SIZES, SECTION_DTYPES))
    out_specs = tuple(pl.BlockSpec((tm, n), lambda i: (i, 0)) for n in SECTION_SIZES)
    return pl.pallas_call(
        functools.partial(_inproj_body, seq_blocks),
        grid=(tokens // tm,),
        in_specs=[
            pl.BlockSpec((tm, D_MODEL), lambda i: (i, 0)),
            pl.BlockSpec((D_MODEL, total), lambda i: (0, 0), pipeline_mode=pl.Buffered(1)),
            pl.BlockSpec((1, total), lambda i: (0, 0)),
            pl.BlockSpec((2, seq_blocks, HEAD_DIM), lambda i: (0, 0, 0)),
            pl.BlockSpec((4, BLOCK, HEAD_DIM), lambda i: (0, 0, 0)),
        ],
        out_specs=out_specs,
        out_shape=out_shape,
        compiler_params=pltpu.CompilerParams(
            dimension_semantics=("arbitrary",), vmem_limit_bytes=VMEM_LIMIT_BYTES),
        name="inproj",
    )(x, w, b, rope_base, rope_off)


def _attn_body(sink_ref, q_ref, kp_ref, kc_ref, kn_ref, vp_ref, vc_ref, vn_ref, g_ref,
               o_ref, kband_ref, vband_ref):
    i = pl.program_id(1)
    last = pl.num_programs(1) - 1
    tq = q_ref.shape[0]
    nblk = tq // BLOCK
    band_w = 3 * BLOCK

    kband_ref[0:BLOCK, :] = kp_ref[...]
    kband_ref[BLOCK:BLOCK + tq, :] = kc_ref[...]
    kband_ref[BLOCK + tq:, :] = kn_ref[...]
    vband_ref[0:BLOCK, :] = vp_ref[...]
    vband_ref[BLOCK:BLOCK + tq, :] = vc_ref[...]
    vband_ref[BLOCK + tq:, :] = vn_ref[...]

    row = lax.broadcasted_iota(jnp.int32, (BLOCK, band_w), 0)
    col = lax.broadcasted_iota(jnp.int32, (BLOCK, band_w), 1)
    in_band = (col >= row + (BLOCK - WINDOW)) & (col <= row + (BLOCK + WINDOW))

    for j in range(nblk):
        valid = in_band
        if j == 0:
            valid = valid & ((col >= BLOCK) | (i > 0))
        if j == nblk - 1:
            valid = valid & ((col < 2 * BLOCK) | (i < last))
        rows = slice(j * BLOCK, (j + 1) * BLOCK)
        for h in range(N_KV_HEADS):
            kj = kband_ref[j * BLOCK:j * BLOCK + band_w, h * HEAD_DIM:(h + 1) * HEAD_DIM]
            vj = vband_ref[j * BLOCK:j * BLOCK + band_w, h * HEAD_DIM:(h + 1) * HEAD_DIM]
            heads = [h * Q_PER_KV + g for g in range(Q_PER_KV)]
            q4 = jnp.concatenate(
                [q_ref[rows, hq * HEAD_DIM:(hq + 1) * HEAD_DIM] for hq in heads], axis=0)
            s4 = lax.dot_general(q4, kj, (((1,), (1,)), ((), ())),
                                 preferred_element_type=F32)
            ps, inv = [], []
            for g, hq in enumerate(heads):
                s = jnp.where(valid, s4[g * BLOCK:(g + 1) * BLOCK], -jnp.inf)
                sk = sink_ref[hq]
                m = jnp.maximum(jnp.max(s, axis=-1, keepdims=True), sk)
                p = jnp.exp(s - m)
                denom = jnp.sum(p, axis=-1, keepdims=True) + jnp.exp(sk - m)
                ps.append(p.astype(BF16))
                inv.append(1.0 / denom)
            o4 = jnp.dot(jnp.concatenate(ps, axis=0), vj, preferred_element_type=F32)
            for g, hq in enumerate(heads):
                cols = slice(hq * HEAD_DIM, (hq + 1) * HEAD_DIM)
                o = o4[g * BLOCK:(g + 1) * BLOCK] * inv[g]
                o_ref[rows, cols] = (o * g_ref[rows, cols].astype(F32)).astype(o_ref.dtype)


def _attention(sink, q, k, v, gate):
    batch, seq = q.shape[0], q.shape[1]
    tq = TQ_ATTN
    per = tq // BLOCK
    n_halo = seq // BLOCK

    def cur(b, i):
        return (b, i, 0)

    def prev(b, i):
        return (b, jnp.maximum(i * per - 1, 0), 0)

    def nxt(b, i):
        return (b, jnp.minimum((i + 1) * per, n_halo - 1), 0)

    kv_cur = pl.BlockSpec((None, tq, D_KV), cur)
    kv_prev = pl.BlockSpec((None, BLOCK, D_KV), prev)
    kv_next = pl.BlockSpec((None, BLOCK, D_KV), nxt)
    return pl.pallas_call(
        _attn_body,
        grid=(batch, seq // tq),
        in_specs=[
            pl.BlockSpec(memory_space=pltpu.SMEM),
            pl.BlockSpec((None, tq, D_ATTN), cur),
            kv_prev, kv_cur, kv_next,
            kv_prev, kv_cur, kv_next,
            pl.BlockSpec((None, tq, D_ATTN), cur),
        ],
        out_specs=pl.BlockSpec((None, tq, D_ATTN), cur),
        out_shape=jax.ShapeDtypeStruct((batch, seq, D_ATTN), BF16),
        scratch_shapes=[pltpu.VMEM((tq + 2 * BLOCK, D_KV), BF16),
                        pltpu.VMEM((tq + 2 * BLOCK, D_KV), BF16)],
        compiler_params=pltpu.CompilerParams(
            dimension_semantics=("arbitrary", "arbitrary"), vmem_limit_bytes=VMEM_LIMIT_BYTES),
        name="band_attention",
    )(sink, q, k, k, k, v, v, v, gate)


def _run(stages):
    for stage in stages:
        stage()


def _interleave(primary, secondary, span=None):
    span = len(primary) if span is None else span
    merged, done = [], 0
    for k, stage in enumerate(primary):
        due = min(((k + 1) * len(secondary)) // span, len(secondary))
        merged.extend(secondary[done:due])
        done = due
        merged.append(stage)
    merged.extend(secondary[done:])
    return merged


def _rnn_stages(reverse, first, last, xp_ref, xc_ref, xn_ref, cw_ref, cb_ref, w_ref,
                ba_ref, bx_ref, lam_ref, emit, ext_ref, a_ref, u_ref, hp_ref, carry_ref):
    tb = xc_ref.shape[0]
    sub_len = tb // SUBLANES
    sub = lax.broadcasted_iota(jnp.int32, (SUBLANES, LANES), 0)
    half_rate = (0.5 * LRU_C) * jax.nn.softplus(-lam_ref[...])
    zero_row = jnp.zeros((1, LANES), F32)

    def gates(n):
        cols = slice(n * LANES, (n + 1) * LANES)
        ext_n = ext_ref.at[n]
        for s in range(SUBLANES):
            for k in range(sub_len // SUBLANES):
                t0 = k * SUBLANES
                v = xc_ref[s * sub_len + t0:s * sub_len + t0 + SUBLANES, cols]
                ext_n[pl.ds((t0 + CONV_LEFT) * SUBLANES + s, SUBLANES, stride=SUBLANES), :] = v
        for j in range(CONV_LEFT):
            before = jnp.where(first, zero_row, xp_ref[SUBLANES - CONV_LEFT + j:SUBLANES - CONV_LEFT + j + 1, cols])
            src = ext_n[(sub_len + j) * SUBLANES:(sub_len + j + 1) * SUBLANES, :]
            ext_n[j * SUBLANES:(j + 1) * SUBLANES, :] = jnp.where(sub == 0, before, pltpu.roll(src, 1, 0))
        for j in range(CONV_RIGHT):
            after = jnp.where(last, zero_row, xn_ref[j:j + 1, cols])
            src = ext_n[(CONV_LEFT + j) * SUBLANES:(CONV_LEFT + j + 1) * SUBLANES, :]
            e = sub_len + CONV_LEFT + j
            ext_n[e * SUBLANES:(e + 1) * SUBLANES, :] = jnp.where(
                sub == SUBLANES - 1, after, pltpu.roll(src, SUBLANES - 1, 0))

        xcn = cb_ref[:, cols]
        for j in range(CONV_W):
            xcn = xcn + cw_ref[j:j + 1, cols] * ext_n[j * SUBLANES:j * SUBLANES + tb, :]
        g = jnp.dot(xcn.astype(BF16), w_ref[n], preferred_element_type=F32)
        t_r = jnp.tanh(g[:, :LANES] + 0.5 * ba_ref[:, cols])
        t_i = jnp.tanh(g[:, LANES:] + 0.5 * bx_ref[:, cols])
        rate = half_rate[:, cols]
        neg_log_a = rate + rate * t_r
        a_ref[n] = jnp.exp(-neg_log_a)
        m = jnp.tanh(neg_log_a)
        half_root = jnp.where(m > 0.0, m * lax.rsqrt(m * (2.0 + 2.0 * m)), 0.0)
        u_ref[n] = half_root * (xcn + xcn * t_i)

    steps = list(range(sub_len - 1, -1, -1) if reverse else range(sub_len))
    step_groups = [steps[g:g + SUBLANES] for g in range(0, sub_len, SUBLANES)]

    def step_rows(t):
        return slice(t * SUBLANES, (t + 1) * SUBLANES)

    slab = (N_RNN_BLOCKS, SUBLANES, LANES)
    state = {}

    def pass1(group, start):
        h = jnp.zeros(slab, F32) if start else state["h"]
        decay = jnp.ones(slab, F32) if start else state["decay"]
        for t in group:
            a_t = a_ref[:, step_rows(t), :]
            h = a_t * h + u_ref[:, step_rows(t), :]
            decay = decay * a_t
        state["h"], state["decay"] = h, decay

    def chain():
        h, decay = state["h"], state["decay"]
        sub3 = lax.broadcasted_iota(jnp.int32, slab, 1)
        d = 1
        while d < SUBLANES:
            keep = (sub3 < SUBLANES - d) if reverse else (sub3 >= d)
            shift = (SUBLANES - d) if reverse else d
            h = jnp.where(keep, h + decay * pltpu.roll(h, shift, 1), h)
            decay = jnp.where(keep, decay * pltpu.roll(decay, shift, 1), decay)
            d *= 2
        carry = carry_ref[...]
        h_end = h + decay * carry
        if reverse:
            h = jnp.where(sub3 == SUBLANES - 1, carry, pltpu.roll(h_end, SUBLANES - 1, 1))
            leaving = h_end[:, 0:1, :]
        else:
            h = jnp.where(sub3 == 0, carry, pltpu.roll(h_end, 1, 1))
            leaving = h_end[:, SUBLANES - 1:SUBLANES, :]
        carry_ref[...] = jnp.broadcast_to(leaving, slab)
        state["h"] = h

    def pass2(group):
        h = state["h"]
        for t in group:
            h = a_ref[:, step_rows(t), :] * h + u_ref[:, step_rows(t), :]
            hp_ref[:, step_rows(t), :] = h
        state["h"] = h

    def relayout(n):
        cols = slice(n * LANES, (n + 1) * LANES)
        hp_n = hp_ref.at[n]
        for s in range(SUBLANES):
            for k in range(sub_len // BF16_SUBLANES):
                t0 = k * BF16_SUBLANES
                parts = [hp_n[pl.ds((t0 + half) * SUBLANES + s, SUBLANES, stride=SUBLANES), :]
                         for half in (0, SUBLANES)]
                r0 = s * sub_len + t0
                emit(slice(r0, r0 + BF16_SUBLANES), cols, jnp.concatenate(parts, axis=0))

    stages = [functools.partial(gates, n) for n in range(N_RNN_BLOCKS)]
    stages += [functools.partial(pass1, group, k == 0) for k, group in enumerate(step_groups)]
    stages.append(chain)
    stages += [functools.partial(pass2, group) for group in step_groups]
    stages += [functools.partial(relayout, n) for n in range(N_RNN_BLOCKS)]
    return stages


def _rnn_scratch(tb):
    slab_rows = lambda rows: pltpu.VMEM((N_RNN_BLOCKS, rows, LANES), F32)
    return [
        slab_rows(tb + (CONV_W - 1) * SUBLANES),
        slab_rows(tb),
        slab_rows(tb),
        slab_rows(tb),
        slab_rows(SUBLANES),
    ]


def _rnn_fwd_body(xp_ref, xc_ref, xn_ref, cw_ref, cb_ref, w_ref, ba_ref, bx_ref, lam_ref,
                  hf_ref, ext_ref, a_ref, u_ref, hp_ref, carry_ref):
    i = pl.program_id(1)
    last = pl.num_programs(1) - 1

    @pl.when(i == 0)
    def _():
        carry_ref[...] = jnp.zeros_like(carry_ref)

    def emit(rows, cols, h):
        hf_ref[rows, cols] = h.astype(hf_ref.dtype)

    _run(_rnn_stages(False, i == 0, i == last, xp_ref, xc_ref, xn_ref, cw_ref, cb_ref, w_ref,
                     ba_ref, bx_ref, lam_ref, emit, ext_ref, a_ref, u_ref, hp_ref, carry_ref))


def _halo_maps(tile_idx, per, n_halo):
    def cur(b, i):
        return (b, tile_idx(i), 0)

    def prev(b, i):
        return (b, jnp.maximum(tile_idx(i) * per - 1, 0), 0)

    def nxt(b, i):
        return (b, jnp.minimum((tile_idx(i) + 1) * per, n_halo - 1), 0)

    return cur, prev, nxt


def _rnn_fwd(x_rnn, conv_w, conv_b, w_gates, ba, bx, lam):
    batch, seq = x_rnn.shape[0], x_rnn.shape[1]
    tb = TB_RNN
    nb = seq // tb
    assert (tb // SUBLANES) % BF16_SUBLANES == 0
    cur, prev, nxt = _halo_maps(lambda i: i, tb // SUBLANES, seq // SUBLANES)

    def const(shape):
        return pl.BlockSpec(shape, lambda b, i: (0,) * len(shape))

    tile = (None, tb, D_RNN)
    halo = (None, SUBLANES, D_RNN)
    return pl.pallas_call(
        _rnn_fwd_body,
        grid=(batch, nb),
        in_specs=[
            pl.BlockSpec(halo, prev), pl.BlockSpec(tile, cur), pl.BlockSpec(halo, nxt),
            const((CONV_W, D_RNN)), const((1, D_RNN)),
            const((N_RNN_BLOCKS, RNN_BLOCK_W, 2 * RNN_BLOCK_W)),
            const((1, D_RNN)), const((1, D_RNN)), const((1, D_RNN)),
        ],
        out_specs=pl.BlockSpec(tile, cur),
        out_shape=jax.ShapeDtypeStruct((batch, seq, D_RNN), BF16),
        scratch_shapes=_rnn_scratch(tb),
        compiler_params=pltpu.CompilerParams(
            dimension_semantics=("arbitrary", "arbitrary"), vmem_limit_bytes=VMEM_LIMIT_BYTES),
        name="rglru_fwd",
    )(x_rnn, x_rnn, x_rnn, conv_w, conv_b, w_gates, ba, bx, lam)


def _bwd_tail_body(alpha, xp_ref, xc_ref, xn_ref, cw_ref, cb_ref, w_ref, ba_ref, bx_ref, lam_ref,
                   x_ref, attn_ref, hf_ref, gr_ref, ma_ref, mr_ref,
                   wa_ref, wr_ref, wo_ref, bo_ref, lg_ref, lb_ref, y_ref,
                   ext_ref, a_ref, u_ref, hp_ref, carry_ref, rnn_ref, merged_ref, z_ref):
    i = pl.program_id(1)
    n_tiles = pl.num_programs(1) - 1

    @pl.when(i == 0)
    def _():
        carry_ref[...] = jnp.zeros_like(carry_ref)
        rnn_ref[1] = jnp.zeros(rnn_ref.shape[1:], rnn_ref.dtype)

    scan_slot = rnn_ref.at[i % 2]
    proj_slot = rnn_ref.at[(i + 1) % 2]

    def merge(c):
        cols = slice(c * MXU_WIDTH, (c + 1) * MXU_WIDTH)
        branch_a = jnp.dot(attn_ref[...], wa_ref[:, cols], preferred_element_type=F32)
        branch_r = jnp.dot(proj_slot[...], wr_ref[:, cols], preferred_element_type=F32)
        merged_ref[:, cols] = (ma_ref[:, cols] * branch_a.astype(BF16)
                               + mr_ref[:, cols] * branch_r.astype(BF16))

    def project(c):
        cols = slice(c * MXU_WIDTH, (c + 1) * MXU_WIDTH)
        out = jnp.dot(merged_ref[...], wo_ref[:, cols], preferred_element_type=F32)
        z_ref[:, cols] = alpha * x_ref[:, cols] + (out + bo_ref[:, cols])

    def norm():
        z = z_ref[...]
        mu = jnp.mean(z, axis=-1, keepdims=True)
        zc = z - mu
        var = jnp.mean(zc * zc, axis=-1, keepdims=True)
        y_ref[...] = zc * lax.rsqrt(var + LN_EPS) * lg_ref[...] + lb_ref[...]

    n_col = D_MODEL // MXU_WIDTH
    proj_stages = [functools.partial(merge, c) for c in range(n_col)]
    proj_stages += [functools.partial(project, c) for c in range(n_col)]
    proj_stages.append(norm)

    def emit(rows, cols, h_bwd):
        h = h_bwd + hf_ref[rows, cols].astype(F32)
        scan_slot[rows, cols] = (h * gr_ref[rows, cols].astype(F32)).astype(rnn_ref.dtype)

    scan_stages = _rnn_stages(
        True, i >= n_tiles - 1, i == 0, xp_ref, xc_ref, xn_ref, cw_ref, cb_ref, w_ref,
        ba_ref, bx_ref, lam_ref, emit, ext_ref, a_ref, u_ref, hp_ref, carry_ref)
    _run(_interleave(scan_stages, proj_stages, span=N_RNN_BLOCKS))


def _bwd_tail(alpha, x_rnn, conv_w, conv_b, w_gates, ba, bx, lam,
              x, attn, hf, gr, ma, mr, wa, wr, wo, bo, lg, lb):
    batch, seq = x_rnn.shape[0], x_rnn.shape[1]
    tb = TB_RNN
    nb = seq // tb
    scan_cur, scan_prev, scan_next = _halo_maps(
        lambda i: jnp.maximum(nb - 1 - i, 0), tb // SUBLANES, seq // SUBLANES)

    def proj_cur(b, i):
        return (b, jnp.minimum(nb - i, nb - 1), 0)

    def const(shape):
        return pl.BlockSpec(shape, lambda b, i: (0,) * len(shape), pipeline_mode=pl.Buffered(1))

    def scan_rows(n):
        return pl.BlockSpec((None, tb, n), scan_cur)

    def proj_rows(n):
        return pl.BlockSpec((None, tb, n), proj_cur)

    halo = (None, SUBLANES, D_RNN)
    return pl.pallas_call(
        functools.partial(_bwd_tail_body, alpha),
        grid=(batch, nb + 1),
        in_specs=[
            pl.BlockSpec(halo, scan_prev), scan_rows(D_RNN), pl.BlockSpec(halo, scan_next),
            const((CONV_W, D_RNN)), const((1, D_RNN)),
            const((N_RNN_BLOCKS, RNN_BLOCK_W, 2 * RNN_BLOCK_W)),
            const((1, D_RNN)), const((1, D_RNN)), const((1, D_RNN)),
            proj_rows(D_MODEL), proj_rows(D_ATTN), scan_rows(D_RNN), scan_rows(D_RNN),
            proj_rows(D_MODEL), proj_rows(D_MODEL),
            const((D_ATTN, D_MODEL)), const((D_RNN, D_MODEL)), const((D_MODEL, D_MODEL)),
            const((1, D_MODEL)), const((1, D_MODEL)), const((1, D_MODEL)),
        ],
        out_specs=proj_rows(D_MODEL),
        out_shape=jax.ShapeDtypeStruct((batch, seq, D_MODEL), F32),
        scratch_shapes=_rnn_scratch(tb) + [
            pltpu.VMEM((2, tb, D_RNN), BF16),
            pltpu.VMEM((tb, D_MODEL), BF16),
            pltpu.VMEM((tb, D_MODEL), F32),
        ],
        compiler_params=pltpu.CompilerParams(
            dimension_semantics=("arbitrary", "arbitrary"), vmem_limit_bytes=VMEM_LIMIT_BYTES),
        name="rglru_bwd_merge_out_norm",
    )(x_rnn, x_rnn, x_rnn, conv_w, conv_b, w_gates, ba, bx, lam,
      x, attn, hf, gr, ma, mr, wa, wr, wo, bo, lg, lb)


def _rope_tables(seq):
    half = HEAD_DIM // 2
    inv = ROPE_THETA ** (-np.arange(half, dtype=np.float64) * (2.0 / HEAD_DIM))
    dup = lambda t: np.concatenate([t, t], axis=-1)
    sign = np.concatenate([-np.ones(half), np.ones(half)])
    base_ang = (BLOCK * np.arange(seq // BLOCK, dtype=np.float64))[:, None] * inv[None, :]
    off_ang = np.arange(BLOCK, dtype=np.float64)[:, None] * inv[None, :]
    base = np.stack([dup(np.cos(base_ang)), dup(np.sin(base_ang))])
    co, so = dup(np.cos(off_ang)), dup(np.sin(off_ang))
    off = np.stack([co, so, sign * co, sign * so])
    return jnp.asarray(base, F32), jnp.asarray(off, F32)


def kernel(x, w_in, b_in, attn_sink, conv_w, conv_b, lru_wa, lru_ba, lru_wx, lru_bx, lru_lambda,
           w_branch_attn, w_branch_rnn, w_out, b_out, ln_gain, ln_bias):
    batch, seq, _ = x.shape
    depth = w_in.shape[0]
    alpha = (2.0 * depth) ** 0.25
    tokens = batch * seq
    rope_base, rope_off = _rope_tables(seq)
    x2 = x.reshape(tokens, D_MODEL)
    for l in range(depth):
        q, k, v, ga, xr, gr, ma, mr = _inproj(
            x2, w_in[l].astype(BF16), b_in[l][None, :], rope_base, rope_off, seq)
        attn = _attention(
            attn_sink[l],
            q.reshape(batch, seq, D_ATTN), k.reshape(batch, seq, D_KV),
            v.reshape(batch, seq, D_KV), ga.reshape(batch, seq, D_ATTN))
        w_gates = (0.5 * jnp.concatenate([lru_wa[l], lru_wx[l]], axis=-1)).astype(BF16)
        xr3 = xr.reshape(batch, seq, D_RNN)
        cw, cb = conv_w[l], conv_b[l][None, :]
        ba, bx, lam = lru_ba[l], lru_bx[l], lru_lambda[l]
        hf = _rnn_fwd(xr3, cw, cb, w_gates[0], ba[0:1], bx[0:1], lam[0:1])
        tok3 = lambda t: t.reshape(batch, seq, t.shape[-1])
        y = _bwd_tail(alpha, xr3, cw, cb, w_gates[1], ba[1:2], bx[1:2], lam[1:2],
                      tok3(x2), attn, hf, tok3(gr), tok3(ma), tok3(mr),
                      w_branch_attn[l].astype(BF16), w_branch_rnn[l].astype(BF16),
                      w_out[l].astype(BF16), b_out[l][None, :], ln_gain[l][None, :],
                      ln_bias[l][None, :])
        x2 = y.reshape(tokens, D_MODEL)
    return x2.reshape(batch, seq, D_MODEL)
```
